```python
import math
import jax
import jax.numpy as jnp
from jax import lax

D_MODEL = 1024
BATCH = 8
SEQ = 2048
DEPTH = 2
DEC_BATCH = 128
DEC_SEQ = 1
PAST_LEN = 2048
PAGE_SIZE = 128

SSD_HEADS = 16
SSD_HEAD_DIM = 64
SSD_WIDTH = SSD_HEADS * SSD_HEAD_DIM
SSD_GROUPS = 2
SSD_STATE = 128
SSD_CONV = 4
SSD_CONV_DIM = SSD_WIDTH + 2 * SSD_GROUPS * SSD_STATE
SSD_CHUNK = 128
SC_WIDTH = 512
SC_CONV = 3
SB_HEADS = 8
SB_HEAD_DIM = 64
SB_WIDTH = SB_HEADS * SB_HEAD_DIM
SB_BLOCK = 128
SB_BIAS_INIT = -6.0
N_BRANCHES = 3
D_FF = 2816
EPS = 1e-6
IN_SPLITS = (N_BRANCHES * D_MODEL, SSD_WIDTH, SSD_CONV_DIM, SSD_HEADS,
             SC_WIDTH, SC_WIDTH, SC_WIDTH, SB_WIDTH, SB_WIDTH, SB_WIDTH)
IN_COLS = sum(IN_SPLITS)

kernel_name = "hybrid_ssd_shortconv_stickbreaking_macaron_step"


def rmsnorm(x, g):
    xf = x.astype(jnp.float32)
    y = xf * lax.rsqrt(jnp.mean(xf * xf, axis=-1, keepdims=True) + EPS)
    return (y * g.astype(jnp.float32)).astype(x.dtype)


def swiglu(x, w_gate, w_up, w_down):
    return (jax.nn.silu(x @ w_gate) * (x @ w_up)) @ w_down


def split_cols(u, sizes):
    parts, start = [], 0
    for size in sizes:
        parts.append(u[..., start:start + size])
        start += size
    return parts


def causal_dwconv(u, w, prev):
    k_width, t = w.shape[0], u.shape[1]
    full = jnp.concatenate([prev.astype(u.dtype), u], axis=1)
    out = sum(full[:, k:k + t] * w[k] for k in range(k_width))
    return out, full[:, full.shape[1] - (k_width - 1):]


def ssd_chunked(xh, dt, a, bm, cm):
    b, t = xh.shape[0], xh.shape[1]
    nc, e = t // SSD_CHUNK, SSD_HEADS // SSD_GROUPS
    xd = (xh * dt[..., None]).reshape(b, nc, SSD_CHUNK, SSD_GROUPS, e, SSD_HEAD_DIM)
    la = (dt * a).reshape(b, nc, SSD_CHUNK, SSD_GROUPS, e)
    bc = bm.reshape(b, nc, SSD_CHUNK, SSD_GROUPS, SSD_STATE)
    cc = cm.reshape(b, nc, SSD_CHUNK, SSD_GROUPS, SSD_STATE)
    cs = jnp.cumsum(la, axis=2)
    causal = jnp.tril(jnp.ones((SSD_CHUNK, SSD_CHUNK), dtype=bool))
    seg = cs[:, :, :, None] - cs[:, :, None, :]
    decay = jnp.exp(jnp.where(causal[:, :, None, None], seg, -jnp.inf))
    cb = jnp.einsum("bclgn,bcsgn->bclsg", cc, bc)
    y_diag = jnp.einsum("bclsg,bclsge,bcsgep->bclgep", cb, decay, xd)
    to_end = jnp.exp(cs[:, :, -1:] - cs)
    chunk_states = jnp.einsum("bclgn,bclge,bclgep->bcgepn", bc, to_end, xd)
    chunk_decay = jnp.exp(cs[:, :, -1])

    def step(h, inp):
        st, dcy = inp
        return h * dcy[..., None, None] + st, h

    h0 = jnp.zeros((b, SSD_GROUPS, e, SSD_HEAD_DIM, SSD_STATE), jnp.float32)
    h_final, h_in = lax.scan(step, h0, (jnp.moveaxis(chunk_states, 1, 0), jnp.moveaxis(chunk_decay, 1, 0)))
    h_in = jnp.moveaxis(h_in, 0, 1)
    y_off = jnp.einsum("bclgn,bcgepn,bclge->bclgep", cc, h_in, jnp.exp(cs))
    y = (y_diag + y_off).reshape(b, t, SSD_HEADS, SSD_HEAD_DIM)
    return y, h_final.reshape(b, SSD_HEADS, SSD_HEAD_DIM, SSD_STATE)


def ssd_recurrent(xh, dt, a, bm, cm, h0):
    b, e = xh.shape[0], SSD_HEADS // SSD_GROUPS

    def step(h, inp):
        x_t, dt_t, b_t, c_t = inp
        da = jnp.exp(dt_t * a).reshape(b, SSD_GROUPS, e)
        xd = (x_t * dt_t[..., None]).reshape(b, SSD_GROUPS, e, SSD_HEAD_DIM)
        h = h * da[..., None, None] + jnp.einsum("bgep,bgn->bgepn", xd, b_t)
        y = jnp.einsum("bgepn,bgn->bgep", h, c_t).reshape(b, SSD_HEADS, SSD_HEAD_DIM)
        return h, y

    xs = tuple(jnp.moveaxis(arr, 1, 0) for arr in (xh, dt, bm, cm))
    h_init = h0.astype(jnp.float32).reshape(b, SSD_GROUPS, e, SSD_HEAD_DIM, SSD_STATE)
    h_final, ys = lax.scan(step, h_init, xs)
    return jnp.moveaxis(ys, 0, 1), h_final.reshape(b, SSD_HEADS, SSD_HEAD_DIM, SSD_STATE)


def ssd_branch(z, xbc, dt_raw, lp, conv_prev, h0):
    bsz, t = xbc.shape[0], xbc.shape[1]
    conv_out, conv_state = causal_dwconv(xbc, lp["ssd_conv_w"], conv_prev)
    act = jax.nn.silu((conv_out + lp["ssd_conv_b"]).astype(jnp.float32))
    xs, bm, cm = split_cols(act, (SSD_WIDTH, SSD_GROUPS * SSD_STATE, SSD_GROUPS * SSD_STATE))
    xh = xs.reshape(bsz, t, SSD_HEADS, SSD_HEAD_DIM)
    bm = bm.reshape(bsz, t, SSD_GROUPS, SSD_STATE)
    cm = cm.reshape(bsz, t, SSD_GROUPS, SSD_STATE)
    dt = jax.nn.softplus(dt_raw.astype(jnp.float32) + lp["ssd_dt_bias"].astype(jnp.float32))
    a = -jnp.exp(lp["ssd_a_log"].astype(jnp.float32))
    if h0 is None:
        y, h_new = ssd_chunked(xh, dt, a, bm, cm)
    else:
        y, h_new = ssd_recurrent(xh, dt, a, bm, cm, h0)
    y = y + xh * lp["ssd_d"].astype(jnp.float32)[:, None]
    y = y.reshape(bsz, t, SSD_WIDTH) * jax.nn.silu(z.astype(jnp.float32))
    y = rmsnorm(y.reshape(bsz, t, SSD_GROUPS, SSD_WIDTH // SSD_GROUPS),
                lp["ssd_norm"].reshape(SSD_GROUPS, SSD_WIDTH // SSD_GROUPS)).reshape(bsz, t, SSD_WIDTH)
    return y.astype(z.dtype), h_new.astype(z.dtype), conv_state


def shortconv_branch(gate_b, gate_c, x_in, w, prev):
    conv_out, conv_state = causal_dwconv(gate_c * x_in, w, prev)
    return gate_b * conv_out, conv_state


def stick_breaking_block(q, k, v, bias, q_start):
    tq, tk = q.shape[1], k.shape[1]
    z = jnp.einsum("bihd,bjhd->bhij", q.astype(jnp.float32), k.astype(jnp.float32)) * (SB_HEAD_DIM ** -0.5)
    z = z + bias.astype(jnp.float32)[None, :, None, None]
    mask = jnp.arange(tk)[None, :] < (q_start + jnp.arange(tq))[:, None]
    log_beta = jax.nn.log_sigmoid(z)
    log_keep = jnp.where(mask, jax.nn.log_sigmoid(-z), 0.0)
    log_tail = lax.cumsum(log_keep, axis=3, reverse=True) - log_keep
    weights = jnp.exp(jnp.where(mask, log_beta + log_tail, -jnp.inf))
    out = jnp.einsum("bhij,bjhd->bihd", weights, v.astype(jnp.float32))
    return out.astype(v.dtype)


def sb_branch(q, k, v, lp, past_k, past_v):
    bsz, t = q.shape[0], q.shape[1]
    q = rmsnorm(q.reshape(bsz, t, SB_HEADS, SB_HEAD_DIM), lp["sb_q_norm"])
    k = rmsnorm(k.reshape(bsz, t, SB_HEADS, SB_HEAD_DIM), lp["sb_k_norm"])
    v = v.reshape(bsz, t, SB_HEADS, SB_HEAD_DIM)
    if past_k is None:
        k_all, v_all, offset = k, v, 0
    else:
        k_all = jnp.concatenate([past_k.astype(k.dtype), k], axis=1)
        v_all = jnp.concatenate([past_v.astype(v.dtype), v], axis=1)
        offset = past_k.shape[1]
    outs = []
    for start in range(0, t, SB_BLOCK):
        stop = min(start + SB_BLOCK, t)
        kv_len = offset + stop
        outs.append(stick_breaking_block(q[:, start:stop], k_all[:, :kv_len], v_all[:, :kv_len],
                                         lp["sb_logit_bias"], offset + start))
    out = jnp.concatenate(outs, axis=1).reshape(bsz, t, SB_WIDTH)
    return out, k, v


def mix_layer(h, lp, past):
    u = h @ lp["w_in"]
    gate_pre, z, xbc, dt_raw, sc_b, sc_c, sc_x, q, k, v = split_cols(u, IN_SPLITS)
    gates = jax.nn.sigmoid((gate_pre + lp["gate_bias"]).astype(jnp.float32)).astype(h.dtype)
    g_a, g_b, g_c = split_cols(gates, (D_MODEL, D_MODEL, D_MODEL))
    bsz = h.shape[0]
    if past is None:
        past_k = past_v = h0 = None
        ssd_prev = jnp.zeros((bsz, SSD_CONV - 1, SSD_CONV_DIM), h.dtype)
        sc_prev = jnp.zeros((bsz, SC_CONV - 1, SC_WIDTH), h.dtype)
    else:
        past_k, past_v, h0, ssd_prev, sc_prev = past
    y_a, ssd_h, ssd_conv = ssd_branch(z, xbc, dt_raw, lp, ssd_prev, h0)
    y_b, sc_conv = shortconv_branch(sc_b, sc_c, sc_x, lp["sc_conv_w"], sc_prev)
    y_c, k_new, v_new = sb_branch(q, k, v, lp, past_k, past_v)
    merged = (g_a * (y_a @ lp["w_proj_a"]) + g_b * (y_b @ lp["w_proj_b"])
              + g_c * (y_c @ lp["w_proj_c"]))
    return merged @ lp["w_out"], (k_new, v_new, ssd_h, ssd_conv, sc_conv)


def block(x, lp, past):
    x = x + 0.5 * swiglu(rmsnorm(x, lp["ffn1_norm"]), lp["ffn1_w_gate"], lp["ffn1_w_up"], lp["ffn1_w_down"])
    m, new_state = mix_layer(rmsnorm(x, lp["mix_norm"]), lp, past)
    x = x + m
    x = x + 0.5 * swiglu(rmsnorm(x, lp["ffn2_norm"]), lp["ffn2_w_gate"], lp["ffn2_w_up"], lp["ffn2_w_down"])
    return x, new_state


def setup_inputs(seed: int = 0) -> dict:
    key = jax.random.key(seed)
    ks = iter(jax.random.split(key, 48))
    f32 = jnp.float32

    def nrm(shape, scale):
        return jax.random.normal(next(ks), shape, f32) * scale

    def gain(shape):
        return 1.0 + nrm(shape, 0.1)

    n_pages = PAST_LEN // PAGE_SIZE
    n_used = DEC_BATCH * n_pages
    n_phys = n_used + (n_used + 3) // 4
    page_table = jax.random.permutation(next(ks), n_phys)[:n_used].reshape(DEC_BATCH, n_pages).astype(jnp.int32)
    dt_init = jnp.exp(jax.random.uniform(next(ks), (DEPTH, SSD_HEADS), f32, math.log(1e-3), math.log(1e-1)))
    return {
        "x_prompt": nrm((BATCH, SEQ, D_MODEL), 1.0),
        "x_sample": nrm((DEC_BATCH, DEC_SEQ, D_MODEL), 1.0),
        "cache_sb_k": nrm((DEPTH, n_phys, PAGE_SIZE, SB_HEADS, SB_HEAD_DIM), 1.0),
        "cache_sb_v": nrm((DEPTH, n_phys, PAGE_SIZE, SB_HEADS, SB_HEAD_DIM), 1.0),
        "page_table": page_table,
        "state_ssd": nrm((DEPTH, DEC_BATCH, SSD_HEADS, SSD_HEAD_DIM, SSD_STATE), 0.1),
        "state_ssd_conv": nrm((DEPTH, DEC_BATCH, SSD_CONV - 1, SSD_CONV_DIM), 1.0),
        "state_sc_conv": nrm((DEPTH, DEC_BATCH, SC_CONV - 1, SC_WIDTH), 1.0),
        "ffn1_norm": gain((DEPTH, D_MODEL)),
        "ffn1_w_gate": nrm((DEPTH, D_MODEL, D_FF), D_MODEL ** -0.5),
        "ffn1_w_up": nrm((DEPTH, D_MODEL, D_FF), D_MODEL ** -0.5),
        "ffn1_w_down": nrm((DEPTH, D_FF, D_MODEL), D_FF ** -0.5),
        "mix_norm": gain((DEPTH, D_MODEL)),
        "w_in": nrm((DEPTH, D_MODEL, IN_COLS), D_MODEL ** -0.5),
        "gate_bias": nrm((DEPTH, N_BRANCHES * D_MODEL), 0.1),
        "ssd_conv_w": nrm((DEPTH, SSD_CONV, SSD_CONV_DIM), SSD_CONV ** -0.5),
        "ssd_conv_b": nrm((DEPTH, SSD_CONV_DIM), 0.1),
        "ssd_dt_bias": dt_init + jnp.log(-jnp.expm1(-dt_init)),
        "ssd_a_log": jnp.log(jax.random.uniform(next(ks), (DEPTH, SSD_HEADS), f32, 1.0, 16.0)),
        "ssd_d": gain((DEPTH, SSD_HEADS)),
        "ssd_norm": gain((DEPTH, SSD_WIDTH)),
        "sc_conv_w": nrm((DEPTH, SC_CONV, SC_WIDTH), SC_CONV ** -0.5),
        "sb_q_norm": gain((DEPTH, SB_HEAD_DIM)),
        "sb_k_norm": gain((DEPTH, SB_HEAD_DIM)),
        "sb_logit_bias": SB_BIAS_INIT + nrm((DEPTH, SB_HEADS), 0.5),
        "w_proj_a": nrm((DEPTH, SSD_WIDTH, D_MODEL), SSD_WIDTH ** -0.5),
        "w_proj_b": nrm((DEPTH, SC_WIDTH, D_MODEL), SC_WIDTH ** -0.5),
        "w_proj_c": nrm((DEPTH, SB_WIDTH, D_MODEL), SB_WIDTH ** -0.5),
        "w_out": nrm((DEPTH, D_MODEL, D_MODEL), D_MODEL ** -0.5),
        "ffn2_norm": gain((DEPTH, D_MODEL)),
        "ffn2_w_gate": nrm((DEPTH, D_MODEL, D_FF), D_MODEL ** -0.5),
        "ffn2_w_up": nrm((DEPTH, D_MODEL, D_FF), D_MODEL ** -0.5),
        "ffn2_w_down": nrm((DEPTH, D_FF, D_MODEL), D_FF ** -0.5),
    }


def reference(x_prompt, x_sample, cache_sb_k, cache_sb_v, page_table, state_ssd, state_ssd_conv,
              state_sc_conv, ffn1_norm, ffn1_w_gate, ffn1_w_up, ffn1_w_down, mix_norm, w_in, gate_bias,
              ssd_conv_w, ssd_conv_b, ssd_dt_bias, ssd_a_log, ssd_d, ssd_norm, sc_conv_w, sb_q_norm,
              sb_k_norm, sb_logit_bias, w_proj_a, w_proj_b, w_proj_c, w_out, ffn2_norm, ffn2_w_gate,
              ffn2_w_up, ffn2_w_down):
    dec_b = x_sample.shape[0]
    past_len = page_table.shape[1] * cache_sb_k.shape[2]
    yp, ys = x_prompt, x_sample
    kp_l, vp_l, ks_l, vs_l = [], [], [], []
    hp_l, hs_l, cp_l, cs_l, sp_l, ss_l = [], [], [], [], [], []
    for l in range(DEPTH):
        lp = {
            "ffn1_norm": ffn1_norm[l], "ffn1_w_gate": ffn1_w_gate[l], "ffn1_w_up": ffn1_w_up[l],
            "ffn1_w_down": ffn1_w_down[l], "mix_norm": mix_norm[l], "w_in": w_in[l],
            "gate_bias": gate_bias[l], "ssd_conv_w": ssd_conv_w[l], "ssd_conv_b": ssd_conv_b[l],
            "ssd_dt_bias": ssd_dt_bias[l], "ssd_a_log": ssd_a_log[l], "ssd_d": ssd_d[l],
            "ssd_norm": ssd_norm[l], "sc_conv_w": sc_conv_w[l], "sb_q_norm": sb_q_norm[l],
            "sb_k_norm": sb_k_norm[l], "sb_logit_bias": sb_logit_bias[l], "w_proj_a": w_proj_a[l],
            "w_proj_b": w_proj_b[l], "w_proj_c": w_proj_c[l], "w_out": w_out[l],
            "ffn2_norm": ffn2_norm[l], "ffn2_w_gate": ffn2_w_gate[l], "ffn2_w_up": ffn2_w_up[l],
            "ffn2_w_down": ffn2_w_down[l],
        }
        yp, (kp, vp, hp, cp, sp) = block(yp, lp, None)
        past_k = cache_sb_k[l][page_table].reshape(dec_b, past_len, SB_HEADS, SB_HEAD_DIM)
        past_v = cache_sb_v[l][page_table].reshape(dec_b, past_len, SB_HEADS, SB_HEAD_DIM)
        past = (past_k, past_v, state_ssd[l], state_ssd_conv[l], state_sc_conv[l])
        ys, (ks, vs, hs, cs, ss) = block(ys, lp, past)
        kp_l.append(kp); vp_l.append(vp); ks_l.append(ks); vs_l.append(vs)
        hp_l.append(hp); hs_l.append(hs); cp_l.append(cp); cs_l.append(cs); sp_l.append(sp); ss_l.append(ss)
    return (yp, ys,
            jnp.stack(kp_l), jnp.stack(vp_l), jnp.stack(ks_l), jnp.stack(vs_l),
            jnp.stack(hp_l), jnp.stack(hs_l), jnp.stack(cp_l), jnp.stack(cs_l),
            jnp.stack(sp_l), jnp.stack(ss_l))
```

```python
import functools

import jax
import jax.numpy as jnp
from jax import lax
from jax.experimental import pallas as pl
from jax.experimental.pallas import tpu as pltpu

F32 = jnp.float32
BF16 = jnp.bfloat16
EPS = 1e-6

LANES = 128
SUBLANES = 8
VMEM_LIMIT = 56 * 1024 * 1024

D_MODEL = 1024
SSD_HEADS = 16
SSD_HEAD_DIM = 64
SSD_WIDTH = SSD_HEADS * SSD_HEAD_DIM
SSD_GROUPS = 2
SSD_STATE = 128
SSD_CONV = 4
SSD_CONV_DIM = SSD_WIDTH + 2 * SSD_GROUPS * SSD_STATE
SSD_GROUP_WIDTH = SSD_WIDTH // SSD_GROUPS
HEADS_PER_GROUP = SSD_HEADS // SSD_GROUPS
SC_WIDTH = 512
SC_CONV = 3
SB_HEADS = 8
SB_HEAD_DIM = 64
SB_WIDTH = SB_HEADS * SB_HEAD_DIM
SB_SCALE = SB_HEAD_DIM ** -0.5
N_GATES = 3 * D_MODEL

C_GATE = 0
C_Z = C_GATE + N_GATES
C_XBC = C_Z + SSD_WIDTH
C_SC = C_XBC + SSD_CONV_DIM
C_Q = C_SC + 3 * SC_WIDTH
C_DT = C_Q + SB_WIDTH
C_END = C_DT + LANES

FFN_TM = 512
FFN_CHUNKS = 2
INPROJ_TM = 256
MERGE_TM = 256
SSD_CHUNK = 128
SB_TQ = 256


def _dot(a, b):
    return jnp.dot(a, b, preferred_element_type=F32)


def _dot_nt(a, b):
    return lax.dot_general(a, b, (((1,), (1,)), ((), ())), preferred_element_type=F32)


def _bf16_pieces(a, n):
    pieces, r = [], a
    for i in range(n):
        p = r.astype(BF16)
        pieces.append(p)
        if i + 1 < n:
            r = r - p.astype(F32)
    return pieces


def _dot_split(a, b01, n):
    out = None
    for p in _bf16_pieces(a, n):
        d = _dot(p, b01)
        out = d if out is None else out + d
    return out


def _dot_split_left(a01, b, n):
    out = None
    for p in _bf16_pieces(b, n):
        d = _dot(a01, p)
        out = d if out is None else out + d
    return out


def _rmsnorm(x, g):
    return x * lax.rsqrt(jnp.mean(x * x, axis=-1, keepdims=True) + EPS) * g


def _silu(x):
    return x * jax.nn.sigmoid(x)


def _softplus(x):
    return jnp.maximum(x, 0.0) + jnp.log1p(jnp.exp(-jnp.abs(x)))


def _const_spec(shape):
    nd = len(shape)
    return pl.BlockSpec(shape, lambda *_: (0,) * nd, pipeline_mode=pl.Buffered(1))


def _params(*sem):
    return pltpu.CompilerParams(dimension_semantics=sem, vmem_limit_bytes=VMEM_LIMIT)


def _ffn_body(x_ref, g_ref, wg_ref, wu_ref, wd_ref, o_ref):
    x = x_ref[...]
    h = _rmsnorm(x, g_ref[...]).astype(BF16)
    c = wg_ref.shape[1] // FFN_CHUNKS
    y = None
    for i in range(FFN_CHUNKS):
        a = _dot(h, wg_ref[:, i * c:(i + 1) * c])
        u = _dot(h, wu_ref[:, i * c:(i + 1) * c])
        act = (_silu(a) * u).astype(BF16)
        yi = _dot(act, wd_ref[i * c:(i + 1) * c, :])
        y = yi if y is None else y + yi
    o_ref[...] = x + 0.5 * y


def _ffn(x, g, wg, wu, wd):
    n, d = x.shape
    tm = min(FFN_TM, n)
    row = pl.BlockSpec((tm, d), lambda i: (i, 0))
    return pl.pallas_call(
        _ffn_body,
        grid=(n // tm,),
        in_specs=[row, _const_spec(g.shape), _const_spec(wg.shape), _const_spec(wu.shape),
                  _const_spec(wd.shape)],
        out_specs=row,
        out_shape=jax.ShapeDtypeStruct((n, d), F32),
        compiler_params=_params("parallel"),
        name="ffn",
    )(x, g, wg, wu, wd)


def _inproj_body(x_ref, g_ref, w_ref, wkv_ref, gb_ref, qn_ref, kn_ref, bd_ref,
                 gates_ref, z_ref, xbc_ref, dt_ref, sc_ref, q_ref, kt_ref, vt_ref):
    h = _rmsnorm(x_ref[...], g_ref[...]).astype(BF16)

    def seg(lo, hi):
        return _dot(h, w_ref[:, lo:hi])

    gates_ref[...] = jax.nn.sigmoid(seg(C_GATE, C_Z) + gb_ref[...])
    z_ref[...] = seg(C_Z, C_XBC)
    xbc_ref[...] = seg(C_XBC, C_SC)
    sc_ref[...] = seg(C_SC, C_Q)
    dt_ref[...] = seg(C_DT, C_END)

    q = seg(C_Q, C_DT)
    bd = bd_ref[...]
    qn = qn_ref[...]
    for s in range(SB_WIDTH // LANES):
        sl = slice(s * LANES, (s + 1) * LANES)
        qs = q[:, sl]
        ms = _dot_split(qs * qs, bd, 2) * (1.0 / SB_HEAD_DIM)
        q_ref[:, sl] = qs * lax.rsqrt(ms + EPS) * qn[:, sl]

    kv_t = _dot_nt(wkv_ref[...], h)
    tm = kv_t.shape[1]
    k_t = kv_t[:SB_WIDTH].reshape(SB_HEADS, SB_HEAD_DIM, tm)
    ms = jnp.mean(k_t * k_t, axis=1, keepdims=True)
    kt_ref[...] = k_t * lax.rsqrt(ms + EPS) * kn_ref[...]
    vt_ref[...] = kv_t[SB_WIDTH:].reshape(SB_HEADS, SB_HEAD_DIM, tm)


def _inproj(x, seq, g, w, wkv_t, gate_bias, qn, kn, bd):
    n, d = x.shape
    tm = min(INPROJ_TM, seq)
    nt = seq // tm

    def row(c):
        return pl.BlockSpec((tm, c), lambda i: (i, 0))

    kv_spec = pl.BlockSpec((None, SB_HEADS, SB_HEAD_DIM, tm), lambda i: (i // nt, 0, 0, i % nt))
    kv_shape = jax.ShapeDtypeStruct((n // seq, SB_HEADS, SB_HEAD_DIM, seq), F32)
    widths = (N_GATES, SSD_WIDTH, SSD_CONV_DIM, LANES, 3 * SC_WIDTH, SB_WIDTH)
    return pl.pallas_call(
        _inproj_body,
        grid=(n // tm,),
        in_specs=[row(d)] + [_const_spec(a.shape) for a in (g, w, wkv_t, gate_bias, qn, kn, bd)],
        out_specs=[row(c) for c in widths] + [kv_spec, kv_spec],
        out_shape=[jax.ShapeDtypeStruct((n, c), F32) for c in widths] + [kv_shape, kv_shape],
        compiler_params=_params("parallel"),
        name="inproj",
    )(x, g, w, wkv_t, gate_bias, qn, kn, bd)


def _shift_rows(x, halo, k):
    r = pltpu.roll(x, k, axis=0)
    hk = pltpu.roll(halo, k, axis=0)
    rid = lax.broadcasted_iota(jnp.int32, hk.shape, 0)
    head = jnp.where(rid < k, hk, r[:SUBLANES])
    return jnp.concatenate([head, r[SUBLANES:]], axis=0)


def _group_norm_store(dst_ref, y, nw):
    for g in range(SSD_GROUPS):
        sl = slice(g * SSD_GROUP_WIDTH, (g + 1) * SSD_GROUP_WIDTH)
        dst_ref[:, sl] = _rmsnorm(y[:, sl], nw[:, sl])


def _seqmix_body(z_ref, xbc_ref, xh_ref, dt_ref, sc_ref, sh_ref,
                 cw_ref, cb_ref, dtb_ref, alog_ref, dexp_ref, nw_ref, scw_ref, e_ref, tri_ref,
                 ya_ref, yb_ref, st_ref, cst_ref, scst_ref, hs_ref):
    t = pl.program_id(1)
    first = t == 0
    last = t == pl.num_programs(1) - 1
    L = xbc_ref.shape[0]

    @pl.when(first)
    def _():
        hs_ref[...] = jnp.zeros_like(hs_ref)

    xbc = xbc_ref[...]
    halo = jnp.where(first, 0.0, xh_ref[...])
    cw = cw_ref[...]
    conv = (_shift_rows(xbc, halo, 3) * cw[0:1] + _shift_rows(xbc, halo, 2) * cw[1:2]
            + _shift_rows(xbc, halo, 1) * cw[2:3] + xbc * cw[3:4])
    act = _silu(conv + cb_ref[...])
    xs = act[:, :SSD_WIDTH]

    dt = _softplus(dt_ref[...] + dtb_ref[...])
    la = dt * (-jnp.exp(alog_ref[...]))
    cs = _dot_split_left(tri_ref[...], la, 3)
    cs_t = cs.T
    tot = cs[L - 1:L, :]
    to_end = jnp.exp(tot - cs)
    stack = jnp.concatenate(
        [dt, jnp.exp(cs), to_end, jnp.broadcast_to(jnp.exp(tot), (SUBLANES, LANES))], axis=0)
    ex = _dot_split(stack, e_ref[...], 3)
    dt_e, ecs_e, te_e, cd_e = ex[:L], ex[L:2 * L], ex[2 * L:3 * L], ex[3 * L:3 * L + 1]
    xd = xs * dt_e
    xd_b = xd.astype(BF16)
    xdte_b = (xd * te_e).astype(BF16)

    rid = lax.broadcasted_iota(jnp.int32, (L, L), 0)
    cid = lax.broadcasted_iota(jnp.int32, (L, L), 1)
    causal = rid >= cid
    low_half = lax.broadcasted_iota(jnp.int32, (L, LANES), 1) < SSD_HEAD_DIM

    y_groups = []
    for g in range(SSD_GROUPS):
        gs = slice(g * SSD_GROUP_WIDTH, (g + 1) * SSD_GROUP_WIDTH)
        b_g = act[:, SSD_WIDTH + g * SSD_STATE:SSD_WIDTH + (g + 1) * SSD_STATE]
        c0 = SSD_WIDTH + SSD_GROUPS * SSD_STATE
        c_g = act[:, c0 + g * SSD_STATE:c0 + (g + 1) * SSD_STATE]
        b_gb, c_gb = b_g.astype(BF16), c_g.astype(BF16)
        cb = _dot_nt(c_gb, b_gb)
        h_in = hs_ref[g]
        y_off = _dot(c_gb, h_in.astype(BF16)) * ecs_e[:, gs]
        pieces = []
        for pr in range(HEADS_PER_GROUP // 2):
            ha = g * HEADS_PER_GROUP + 2 * pr
            xp = xd_b[:, ha * SSD_HEAD_DIM:(ha + 2) * SSD_HEAD_DIM]
            yd = []
            for hh in (ha, ha + 1):
                seg = cs[:, hh:hh + 1] - cs_t[hh:hh + 1, :]
                m = cb * jnp.exp(jnp.where(causal, seg, -jnp.inf))
                yd.append(_dot(m.astype(BF16), xp))
            pieces.append(jnp.where(low_half, yd[0], yd[1]))
        y_groups.append(jnp.concatenate(pieces, axis=1) + y_off)
        s_new = _dot(b_g.T.astype(BF16), xdte_b[:, gs])
        hs_ref[g] = h_in * cd_e[:, gs] + s_new

    y = jnp.concatenate(y_groups, axis=1) + xs * dexp_ref[...]
    y = y * _silu(z_ref[...])
    _group_norm_store(ya_ref, y, nw_ref[...])

    sc = sc_ref[...]
    p = sc[:, SC_WIDTH:2 * SC_WIDTH] * sc[:, 2 * SC_WIDTH:]
    sh = sh_ref[...]
    ph = jnp.where(first, 0.0, sh[:, SC_WIDTH:2 * SC_WIDTH] * sh[:, 2 * SC_WIDTH:])
    scw = scw_ref[...]
    yb_ref[...] = sc[:, :SC_WIDTH] * (
        _shift_rows(p, ph, 2) * scw[0:1] + _shift_rows(p, ph, 1) * scw[1:2] + p * scw[2:3])

    @pl.when(last)
    def _():
        for g in range(SSD_GROUPS):
            st_ref[g * HEADS_PER_GROUP:(g + 1) * HEADS_PER_GROUP] = hs_ref[g].T.reshape(
                HEADS_PER_GROUP, SSD_HEAD_DIM, SSD_STATE)
        cst_ref[...] = xbc[L - SUBLANES:]
        scst_ref[...] = p[L - SUBLANES:]


def _seqmix(z, xbc, dt, sc, cw, cb, dtb, alog, dexp, nw, scw, e_mat, tri):
    b, t, _ = z.shape
    L = SSD_CHUNK
    hb = L // SUBLANES

    def cur(c):
        return pl.BlockSpec((None, L, c), lambda i, j: (i, j, 0))

    def halo(c):
        return pl.BlockSpec((None, SUBLANES, c), lambda i, j: (i, jnp.maximum(j * hb - 1, 0), 0))

    def per_seq(shape):
        return pl.BlockSpec((None,) + shape, lambda i, j: (i,) + (0,) * len(shape))

    consts = (cw, cb, dtb, alog, dexp, nw, scw, e_mat, tri)
    return pl.pallas_call(
        _seqmix_body,
        grid=(b, t // L),
        in_specs=[cur(SSD_WIDTH), cur(SSD_CONV_DIM), halo(SSD_CONV_DIM), cur(LANES),
                  cur(3 * SC_WIDTH), halo(3 * SC_WIDTH)] + [_const_spec(a.shape) for a in consts],
        out_specs=[cur(SSD_WIDTH), cur(SC_WIDTH), per_seq((SSD_HEADS, SSD_HEAD_DIM, SSD_STATE)),
                   per_seq((SUBLANES, SSD_CONV_DIM)), per_seq((SUBLANES, SC_WIDTH))],
        out_shape=[jax.ShapeDtypeStruct((b, t, SSD_WIDTH), F32),
                   jax.ShapeDtypeStruct((b, t, SC_WIDTH), F32),
                   jax.ShapeDtypeStruct((b, SSD_HEADS, SSD_HEAD_DIM, SSD_STATE), F32),
                   jax.ShapeDtypeStruct((b, SUBLANES, SSD_CONV_DIM), F32),
                   jax.ShapeDtypeStruct((b, SUBLANES, SC_WIDTH), F32)],
        scratch_shapes=[pltpu.VMEM((SSD_GROUPS, SSD_STATE, SSD_GROUP_WIDTH), F32)],
        compiler_params=_params("parallel", "arbitrary"),
        name="seqmix",
    )(z, xbc, xbc, dt, sc, sc, *consts)


def _sb_body(bias_ref, q_ref, k_ref, v_ref, u_ref, o_ref):
    hp = pl.program_id(1)
    i = pl.program_id(2)
    tq = q_ref.shape[0]
    q2 = q_ref[...]
    low_half = lax.broadcasted_iota(jnp.int32, (tq, LANES), 1) < SB_HEAD_DIM
    q_heads = (jnp.where(low_half, q2, 0.0).astype(BF16), jnp.where(low_half, 0.0, q2).astype(BF16))
    biases = (bias_ref[2 * hp], bias_ref[2 * hp + 1])
    rid = lax.broadcasted_iota(jnp.int32, (tq, tq), 0)
    cid = lax.broadcasted_iota(jnp.int32, (tq, tq), 1)
    u = u_ref[...]

    def head(qh, bh, k2, v2, c, mask):
        z = _dot(qh, k2) * SB_SCALE + bh
        sp = _softplus(z)
        lk = jnp.where(mask, -sp, 0.0)
        tail = _dot_split(lk, u, 2) + c
        w = jnp.where(mask, jnp.exp((z - sp) + tail), 0.0)
        return _dot_nt(w.astype(BF16), v2), c + jnp.sum(lk, axis=1, keepdims=True)

    def body(jj, carry):
        acc, ca, cb = carry
        j = i - jj
        off = pl.multiple_of(j * tq, tq)
        k2 = k_ref[:, :, pl.ds(off, tq)].reshape(LANES, tq).astype(BF16)
        v2 = v_ref[:, :, pl.ds(off, tq)].reshape(LANES, tq).astype(BF16)
        mask = cid + (j - i) * tq < rid
        pa, ca = head(q_heads[0], biases[0], k2, v2, ca, mask)
        pb, cb = head(q_heads[1], biases[1], k2, v2, cb, mask)
        return acc + jnp.where(low_half, pa, pb), ca, cb

    init = (jnp.zeros((tq, LANES), F32), jnp.zeros((tq, 1), F32), jnp.zeros((tq, 1), F32))
    acc, _, _ = lax.fori_loop(0, i + 1, body, init)
    o_ref[...] = acc


def _sb_attn(q, k_t, v_t, bias, u):
    b, t, w = q.shape
    tq = SB_TQ
    qspec = pl.BlockSpec((None, tq, LANES), lambda bi, hp, i: (bi, i, hp))
    kvspec = pl.BlockSpec((None, 2, SB_HEAD_DIM, t), lambda bi, hp, i: (bi, hp, 0, 0))
    return pl.pallas_call(
        _sb_body,
        grid=(b, w // LANES, t // tq),
        in_specs=[pl.BlockSpec(memory_space=pltpu.SMEM), qspec, kvspec, kvspec,
                  _const_spec(u.shape)],
        out_specs=qspec,
        out_shape=jax.ShapeDtypeStruct((b, t, w), F32),
        compiler_params=_params("parallel", "parallel", "arbitrary"),
        name="sb_attn",
    )(bias, q, k_t, v_t, u)


def _merge_body(x_ref, gates_ref, ya_ref, yb_ref, yc_ref, wa_ref, wb_ref, wc_ref, wo_ref, o_ref):
    gates = gates_ref[...]
    m = (gates[:, :D_MODEL] * _dot(ya_ref[...].astype(BF16), wa_ref[...])
         + gates[:, D_MODEL:2 * D_MODEL] * _dot(yb_ref[...].astype(BF16), wb_ref[...])
         + gates[:, 2 * D_MODEL:] * _dot(yc_ref[...].astype(BF16), wc_ref[...]))
    o_ref[...] = x_ref[...] + _dot(m.astype(BF16), wo_ref[...])


def _merge(x, gates, ya, yb, yc, wa, wb, wc, wo):
    n, d = x.shape
    tm = min(MERGE_TM, n)

    def row(c):
        return pl.BlockSpec((tm, c), lambda i: (i, 0))

    return pl.pallas_call(
        _merge_body,
        grid=(n // tm,),
        in_specs=[row(d), row(N_GATES), row(SSD_WIDTH), row(SC_WIDTH), row(SB_WIDTH)]
        + [_const_spec(a.shape) for a in (wa, wb, wc, wo)],
        out_specs=row(d),
        out_shape=jax.ShapeDtypeStruct((n, d), F32),
        compiler_params=_params("parallel"),
        name="merge",
    )(x, gates, ya, yb, yc, wa, wb, wc, wo)


def _dec_ssd_body(z_ref, xbc_ref, dt_ref, sc_ref, cst_ref, scst_ref, st_ref,
                  cw_ref, cb_ref, dtb_ref, alog_ref, dexp_ref, nw_ref, scw_ref,
                  ya_ref, yb_ref, ncst_ref, nscst_ref, nst_ref,
                  xs_s, xdt_s, da_s, dab_s, bm_s, cm_s, yt_s):
    h = pl.program_id(0)
    nb = z_ref.shape[0]

    @pl.when(h == 0)
    def _():
        xbc = xbc_ref[...]
        cw = cw_ref[...]
        conv = (cst_ref[0] * cw[0:1] + cst_ref[1] * cw[1:2] + cst_ref[2] * cw[2:3]
                + xbc * cw[3:4])
        ncst_ref[0] = cst_ref[1]
        ncst_ref[1] = cst_ref[2]
        ncst_ref[2] = xbc
        act = _silu(conv + cb_ref[...])
        xs = act[:, :SSD_WIDTH]
        xs_s[...] = xs
        c0 = SSD_WIDTH + SSD_GROUPS * SSD_STATE
        for g in range(SSD_GROUPS):
            bm_s[g] = act[:, SSD_WIDTH + g * SSD_STATE:SSD_WIDTH + (g + 1) * SSD_STATE]
            cm_s[g] = act[:, c0 + g * SSD_STATE:c0 + (g + 1) * SSD_STATE]
        dt = _softplus(dt_ref[...] + dtb_ref[...])
        da_s[...] = jnp.exp(dt * (-jnp.exp(alog_ref[...])))
        dt_t = dt.T
        xs_t = xs.T
        for hh in range(SSD_HEADS):
            xdt_s[hh] = xs_t[hh * SSD_HEAD_DIM:(hh + 1) * SSD_HEAD_DIM] * dt_t[hh:hh + 1, :]
        sc = sc_ref[...]
        p = sc[:, SC_WIDTH:2 * SC_WIDTH] * sc[:, 2 * SC_WIDTH:]
        scw = scw_ref[...]
        yb_ref[...] = sc[:, :SC_WIDTH] * (
            scst_ref[0] * scw[0:1] + scst_ref[1] * scw[1:2] + p * scw[2:3])
        nscst_ref[0] = scst_ref[1]
        nscst_ref[1] = p

    lane_t = lax.broadcasted_iota(jnp.int32, (nb, LANES), 1)
    da_col = jnp.sum(jnp.where(lane_t == h, da_s[...], 0.0), axis=1, keepdims=True)
    dab_s[...] = jnp.broadcast_to(da_col, (nb, LANES))

    g = h // HEADS_PER_GROUP
    xt = xdt_s[h]
    lane = lax.broadcasted_iota(jnp.int32, (SSD_HEAD_DIM, LANES), 1)
    yacc = jnp.zeros((SSD_HEAD_DIM, LANES), F32)
    for b in range(nb):
        hn = st_ref[b] * dab_s[b:b + 1, :] + xt[:, b:b + 1] * bm_s[g, b:b + 1, :]
        nst_ref[b] = hn
        ycol = jnp.sum(hn * cm_s[g, b:b + 1, :], axis=1, keepdims=True)
        yacc = jnp.where(lane == b, ycol, yacc)
    yt_s[h] = yacc

    @pl.when(h == pl.num_programs(0) - 1)
    def _():
        xs = xs_s[...]
        y = yt_s[...].reshape(SSD_WIDTH, nb).T + xs * dexp_ref[...]
        y = y * _silu(z_ref[...])
        _group_norm_store(ya_ref, y, nw_ref[...])


def _dec_ssd(z, xbc, dt, sc, cst, scst, state, cw, cb, dtb, alog, dexp, nw, scw):
    nb = z.shape[0]
    consts = (cw, cb, dtb, alog, dexp, nw, scw)
    st_spec = pl.BlockSpec((nb, None, SSD_HEAD_DIM, SSD_STATE), lambda h: (0, h, 0, 0))
    return pl.pallas_call(
        _dec_ssd_body,
        grid=(SSD_HEADS,),
        in_specs=[_const_spec(a.shape) for a in (z, xbc, dt, sc, cst, scst)] + [st_spec]
        + [_const_spec(a.shape) for a in consts],
        out_specs=[pl.BlockSpec((nb, SSD_WIDTH), lambda h: (0, 0)),
                   pl.BlockSpec((nb, SC_WIDTH), lambda h: (0, 0)),
                   pl.BlockSpec(cst.shape, lambda h: (0, 0, 0)),
                   pl.BlockSpec(scst.shape, lambda h: (0, 0, 0)),
                   st_spec],
        out_shape=[jax.ShapeDtypeStruct((nb, SSD_WIDTH), F32),
                   jax.ShapeDtypeStruct((nb, SC_WIDTH), F32),
                   jax.ShapeDtypeStruct(cst.shape, F32),
                   jax.ShapeDtypeStruct(scst.shape, F32),
                   jax.ShapeDtypeStruct(state.shape, F32)],
        scratch_shapes=[pltpu.VMEM((nb, SSD_WIDTH), F32),
                        pltpu.VMEM((SSD_HEADS, SSD_HEAD_DIM, nb), F32),
                        pltpu.VMEM((nb, LANES), F32),
                        pltpu.VMEM((nb, LANES), F32),
                        pltpu.VMEM((SSD_GROUPS, nb, SSD_STATE), F32),
                        pltpu.VMEM((SSD_GROUPS, nb, SSD_STATE), F32),
                        pltpu.VMEM((SSD_HEADS, SSD_HEAD_DIM, nb), F32)],
        compiler_params=_params("arbitrary"),
        name="dec_ssd",
    )(z, xbc, dt, sc, cst, scst, state, *consts)


def _dec_attn_body(pt_ref, qt_ref, bias_ref, u_ref, p_ref, *refs):
    n_pages = (len(refs) - 2) // 2
    k_refs, v_refs = refs[:n_pages], refs[n_pages:2 * n_pages]
    o_ref, w_s = refs[2 * n_pages], refs[2 * n_pages + 1]
    b = pl.program_id(0)
    nb = qt_ref.shape[1]

    @pl.when(b == 0)
    def _():
        o_ref[...] = jnp.zeros_like(o_ref)

    token = lax.broadcasted_iota(jnp.int32, (SB_WIDTH, nb), 1) == b
    q_col = jnp.sum(jnp.where(token, qt_ref[...], 0.0), axis=1, keepdims=True)
    sub = lax.broadcasted_iota(jnp.int32, (SB_HEADS, LANES), 0)
    z_pages = []
    for p in range(n_pages):
        zp = jnp.zeros((SB_HEADS, LANES), F32)
        for h in range(SB_HEADS):
            prod = k_refs[p][h] * q_col[h * SB_HEAD_DIM:(h + 1) * SB_HEAD_DIM]
            zp = jnp.where(sub == h, jnp.sum(prod, axis=0, keepdims=True), zp)
        z_pages.append(zp)
    z = jnp.concatenate(z_pages, axis=0) * SB_SCALE + bias_ref[...]
    sp = _softplus(z)
    lk = -sp
    n = z.shape[0]
    tot = jnp.broadcast_to(jnp.sum(lk, axis=1, keepdims=True), (n, LANES))
    tail = _dot_split(lk, u_ref[...], 3) + _dot_split_left(p_ref[...], tot, 3)
    w_s[...] = jnp.exp((z - sp) + tail)
    cols = []
    for h in range(SB_HEADS):
        acc = jnp.zeros((SB_HEAD_DIM, LANES), F32)
        for p in range(n_pages):
            r = p * SB_HEADS + h
            acc = acc + v_refs[p][h] * w_s[r:r + 1, :]
        cols.append(jnp.sum(acc, axis=1, keepdims=True))
    o_col = jnp.concatenate(cols, axis=0)
    o_ref[...] = jnp.where(token, o_col, o_ref[...])


def _dec_attn(page_table, q_t, cache_kt, cache_vt, layer, bias_col, u, pmat):
    nb, n_pages = page_table.shape
    page = cache_kt.shape[-1]

    def page_spec(j):
        return pl.BlockSpec((None, None, SB_HEADS, SB_HEAD_DIM, page),
                            lambda b, pt: (layer, pt[b, j], 0, 0, 0))

    def const(a):
        return pl.BlockSpec(a.shape, lambda b, pt: (0,) * a.ndim, pipeline_mode=pl.Buffered(1))

    grid_spec = pltpu.PrefetchScalarGridSpec(
        num_scalar_prefetch=1,
        grid=(nb,),
        in_specs=[const(q_t), const(bias_col), const(u), const(pmat)]
        + [page_spec(j) for j in range(n_pages)] * 2,
        out_specs=pl.BlockSpec((SB_WIDTH, nb), lambda b, pt: (0, 0)),
        scratch_shapes=[pltpu.VMEM((n_pages * SB_HEADS, LANES), F32)],
    )
    return pl.pallas_call(
        _dec_attn_body,
        grid_spec=grid_spec,
        out_shape=jax.ShapeDtypeStruct((SB_WIDTH, nb), F32),
        compiler_params=_params("arbitrary"),
        name="dec_attn",
    )(page_table, q_t, bias_col, u, pmat, *([cache_kt] * n_pages), *([cache_vt] * n_pages))


def _tri01(n, strict_upper_rows):
    r = lax.broadcasted_iota(jnp.int32, (n, n), 0)
    c = lax.broadcasted_iota(jnp.int32, (n, n), 1)
    return ((r > c) if strict_upper_rows else (r >= c)).astype(BF16)


def kernel(x_prompt, x_sample, cache_sb_k, cache_sb_v, page_table, state_ssd, state_ssd_conv, state_sc_conv, ffn1_norm, ffn1_w_gate, ffn1_w_up, ffn1_w_down, mix_norm, w_in, gate_bias, ssd_conv_w, ssd_conv_b, ssd_dt_bias, ssd_a_log, ssd_d, ssd_norm, sc_conv_w, sb_q_norm, sb_k_norm, sb_logit_bias, w_proj_a, w_proj_b, w_proj_c, w_out, ffn2_norm, ffn2_w_gate, ffn2_w_up, ffn2_w_down):
    depth = w_in.shape[0]
    bsz, seq, d = x_prompt.shape
    dec_b = x_sample.shape[0]
    page = cache_sb_k.shape[2]
    n_pages = page_table.shape[1]
    cache_kt = cache_sb_k.transpose(0, 1, 3, 4, 2)
    cache_vt = cache_sb_v.transpose(0, 1, 3, 4, 2)

    tri_incl = _tri01(SSD_CHUNK, False)
    u_tq = _tri01(SB_TQ, True)
    u_page = _tri01(page, True)
    lane = jnp.arange(LANES)
    bd = ((lane[:, None] // SB_HEAD_DIM) == (lane[None, :] // SB_HEAD_DIM)).astype(BF16)
    col = jnp.arange(SSD_WIDTH)
    e_mat = (lane[:, None] == (col[None, :] // SSD_HEAD_DIM)).astype(BF16)
    r = jnp.arange(n_pages * SB_HEADS)
    pmat = ((r[:, None] % SB_HEADS == r[None, :] % SB_HEADS)
            & (r[None, :] // SB_HEADS > r[:, None] // SB_HEADS)).astype(BF16)

    def pad_lanes(a):
        return jnp.pad(a, ((0, 0), (0, LANES - a.shape[1])))

    xp = x_prompt.reshape(bsz * seq, d)
    xs = x_sample.reshape(dec_b * x_sample.shape[1], d)
    outs = [[] for _ in range(10)]
    for l in range(depth):
        wl = w_in[l]
        c_dt = N_GATES + SSD_WIDTH + SSD_CONV_DIM
        c_q = c_dt + SSD_HEADS + 3 * SC_WIDTH
        w_packed = jnp.concatenate(
            [wl[:, :c_dt], wl[:, c_dt + SSD_HEADS:c_q + SB_WIDTH],
             pad_lanes(wl[:, c_dt:c_dt + SSD_HEADS])], axis=1).astype(BF16)
        wkv_t = wl[:, c_q + SB_WIDTH:].T.astype(BF16)
        f1 = (ffn1_norm[l][None], ffn1_w_gate[l].astype(BF16), ffn1_w_up[l].astype(BF16),
              ffn1_w_down[l].astype(BF16))
        f2 = (ffn2_norm[l][None], ffn2_w_gate[l].astype(BF16), ffn2_w_up[l].astype(BF16),
              ffn2_w_down[l].astype(BF16))
        ip = (mix_norm[l][None], w_packed, wkv_t, gate_bias[l][None],
              jnp.tile(sb_q_norm[l], SB_HEADS)[None], sb_k_norm[l][:, None], bd)
        ssd_p = (ssd_conv_w[l], ssd_conv_b[l][None], pad_lanes(ssd_dt_bias[l][None]),
                 pad_lanes(ssd_a_log[l][None]), jnp.repeat(ssd_d[l], SSD_HEAD_DIM)[None],
                 ssd_norm[l][None], sc_conv_w[l])
        mg = (w_proj_a[l].astype(BF16), w_proj_b[l].astype(BF16), w_proj_c[l].astype(BF16),
              w_out[l].astype(BF16))

        x1 = _ffn(xp, *f1)
        gates, z, xbc, dt, sc, q, k_t, v_t = _inproj(x1, seq, *ip)
        as_seq = lambda a: a.reshape(bsz, seq, a.shape[-1])
        ya, yb, hst, cst8, scst8 = _seqmix(as_seq(z), as_seq(xbc), as_seq(dt), as_seq(sc),
                                            *ssd_p, e_mat, tri_incl)
        yc = _sb_attn(as_seq(q), k_t, v_t, sb_logit_bias[l], u_tq)
        x2 = _merge(x1, gates, ya.reshape(-1, SSD_WIDTH), yb.reshape(-1, SC_WIDTH),
                    yc.reshape(-1, SB_WIDTH), *mg)
        xp = _ffn(x2, *f2)
        outs[0].append(k_t)
        outs[1].append(v_t)
        outs[4].append(hst)
        outs[6].append(cst8[:, SUBLANES - (SSD_CONV - 1):])
        outs[8].append(scst8[:, SUBLANES - (SC_CONV - 1):])

        s1 = _ffn(xs, *f1)
        gates, z, xbc, dt, sc, q, k_t, v_t = _inproj(s1, dec_b, *ip)
        ya, yb, ncst, nscst, nst = _dec_ssd(
            z, xbc, dt, sc, state_ssd_conv[l].transpose(1, 0, 2), state_sc_conv[l].transpose(1, 0, 2),
            state_ssd[l], *ssd_p)
        yc_t = _dec_attn(page_table, q.T, cache_kt, cache_vt, l,
                         jnp.tile(sb_logit_bias[l], n_pages)[:, None], u_page, pmat)
        s2 = _merge(s1, gates, ya, yb, yc_t.T, *mg)
        xs = _ffn(s2, *f2)
        outs[2].append(k_t)
        outs[3].append(v_t)
        outs[5].append(nst)
        outs[7].append(ncst.transpose(1, 0, 2))
        outs[9].append(nscst.transpose(1, 0, 2))

    stacked = [jnp.stack(o) for o in outs]
    kp, vp = (a.transpose(0, 1, 4, 2, 3) for a in stacked[:2])
    ks, vs = (a.transpose(0, 4, 1, 2, 3) for a in stacked[2:4])
    return (xp.reshape(bsz, seq, d), xs.reshape(dec_b, x_sample.shape[1], d),
            kp, vp, ks, vs, stacked[4], stacked[5], stacked[6], stacked[7], stacked[8], stacked[9])
```

```python
import functools

import jax
import jax.numpy as jnp
from jax import lax
from jax.experimental import pallas as pl
from jax.experimental.pallas import tpu as pltpu

F32 = jnp.float32
BF16 = jnp.bfloat16
EPS = 1e-6

LANES = 128
SUBLANES = 8
VMEM_LIMIT = 56 * 1024 * 1024

D_MODEL = 1024
SSD_HEADS = 16
SSD_HEAD_DIM = 64
SSD_WIDTH = SSD_HEADS * SSD_HEAD_DIM
SSD_GROUPS = 2
SSD_STATE = 128
SSD_CONV = 4
SSD_CONV_DIM = SSD_WIDTH + 2 * SSD_GROUPS * SSD_STATE
SSD_GROUP_WIDTH = SSD_WIDTH // SSD_GROUPS
HEADS_PER_GROUP = SSD_HEADS // SSD_GROUPS
SC_WIDTH = 512
SC_CONV = 3
SB_HEADS = 8
SB_HEAD_DIM = 64
SB_WIDTH = SB_HEADS * SB_HEAD_DIM
SB_SCALE = SB_HEAD_DIM ** -0.5
LOG2E = 1.4426950408889634
N_GATES = 3 * D_MODEL

C_GATE = 0
C_Z = C_GATE + N_GATES
C_XBC = C_Z + SSD_WIDTH
C_SC = C_XBC + SSD_CONV_DIM
C_Q = C_SC + 3 * SC_WIDTH
C_DT = C_Q + SB_WIDTH
C_END = C_DT + LANES

FFN_TM = 512
FFN_CHUNKS = 2
INPROJ_TM = 256
MERGE_TM = 256
SSD_CHUNK = 128
SB_TQ = 256


def _dot(a, b):
    return jnp.dot(a, b, preferred_element_type=F32)


def _dot_nt(a, b):
    return lax.dot_general(a, b, (((1,), (1,)), ((), ())), preferred_element_type=F32)


def _bf16_pieces(a, n):
    pieces, r = [], a
    for i in range(n):
        p = r.astype(BF16)
        pieces.append(p)
        if i + 1 < n:
            r = r - p.astype(F32)
    return pieces


def _dot_split(a, b01, n):
    out = None
    for p in _bf16_pieces(a, n):
        d = _dot(p, b01)
        out = d if out is None else out + d
    return out


def _dot_split_left(a01, b, n):
    out = None
    for p in _bf16_pieces(b, n):
        d = _dot(a01, p)
        out = d if out is None else out + d
    return out


def _rmsnorm(x, g):
    return x * lax.rsqrt(jnp.mean(x * x, axis=-1, keepdims=True) + EPS) * g


def _silu(x):
    return x * jax.nn.sigmoid(x)


def _softplus(x):
    return jnp.maximum(x, 0.0) + jnp.log1p(jnp.exp(-jnp.abs(x)))


def _const_spec(shape):
    nd = len(shape)
    return pl.BlockSpec(shape, lambda *_: (0,) * nd, pipeline_mode=pl.Buffered(1))


def _params(*sem):
    return pltpu.CompilerParams(dimension_semantics=sem, vmem_limit_bytes=VMEM_LIMIT)


def _ffn_body(x_ref, g_ref, wg_ref, wu_ref, wd_ref, o_ref):
    x = x_ref[...]
    h = _rmsnorm(x, g_ref[...]).astype(BF16)
    c = wg_ref.shape[1] // FFN_CHUNKS
    y = None
    for i in range(FFN_CHUNKS):
        a = _dot(h, wg_ref[:, i * c:(i + 1) * c])
        u = _dot(h, wu_ref[:, i * c:(i + 1) * c])
        act = (_silu(a) * u).astype(BF16)
        yi = _dot(act, wd_ref[i * c:(i + 1) * c, :])
        y = yi if y is None else y + yi
    o_ref[...] = x + 0.5 * y


def _ffn(x, g, wg, wu, wd):
    n, d = x.shape
    tm = min(FFN_TM, n)
    row = pl.BlockSpec((tm, d), lambda i: (i, 0))
    return pl.pallas_call(
        _ffn_body,
        grid=(n // tm,),
        in_specs=[row, _const_spec(g.shape), _const_spec(wg.shape), _const_spec(wu.shape),
                  _const_spec(wd.shape)],
        out_specs=row,
        out_shape=jax.ShapeDtypeStruct((n, d), F32),
        compiler_params=_params("parallel"),
        name="ffn",
    )(x, g, wg, wu, wd)


def _inproj_body(x_ref, g_ref, w_ref, wkv_ref, gb_ref, qn_ref, kn_ref, bd_ref, *rest):
    gates_ref, z_ref, xbc_ref, dt_ref, sc_ref, q_ref, kt_ref, vt_ref = rest[-8:]
    h = _rmsnorm(x_ref[...], g_ref[...]).astype(BF16)

    def seg(lo, hi):
        return _dot(h, w_ref[:, lo:hi])

    gates_ref[...] = jax.nn.sigmoid(seg(C_GATE, C_Z) + gb_ref[...])
    z_ref[...] = seg(C_Z, C_XBC)
    xbc_ref[...] = seg(C_XBC, C_SC)
    sc_ref[...] = seg(C_SC, C_Q)
    dt_ref[...] = seg(C_DT, C_END)

    q = seg(C_Q, C_DT)
    bd = bd_ref[...]
    qn = qn_ref[...]
    for s in range(SB_WIDTH // LANES):
        sl = slice(s * LANES, (s + 1) * LANES)
        qs = q[:, sl]
        ms = _dot_split(qs * qs, bd, 2) * (1.0 / SB_HEAD_DIM)
        q_ref[:, sl] = qs * lax.rsqrt(ms + EPS) * qn[:, sl]

    kv_t = _dot_nt(wkv_ref[...], h)
    tm = kv_t.shape[1]
    k_t = kv_t[:SB_WIDTH].reshape(SB_HEADS, SB_HEAD_DIM, tm)
    ms = jnp.mean(k_t * k_t, axis=1, keepdims=True)
    kt_ref[...] = k_t * lax.rsqrt(ms + EPS) * kn_ref[...]
    vt_ref[...] = kv_t[SB_WIDTH:].reshape(SB_HEADS, SB_HEAD_DIM, tm)


def _inproj(x, seq, depth, layer, prev_kv, g, w, wkv_t, gate_bias, qn, kn, bd):
    n, d = x.shape
    tm = min(INPROJ_TM, seq)
    nt = seq // tm

    def row(c):
        return pl.BlockSpec((tm, c), lambda i: (i, 0))

    kv_spec = pl.BlockSpec((None, None, SB_HEADS, SB_HEAD_DIM, tm),
                           lambda i: (layer, i // nt, 0, 0, i % nt))
    kv_shape = jax.ShapeDtypeStruct((depth, n // seq, SB_HEADS, SB_HEAD_DIM, seq), F32)
    widths = (N_GATES, SSD_WIDTH, SSD_CONV_DIM, LANES, 3 * SC_WIDTH, SB_WIDTH)
    consts = (g, w, wkv_t, gate_bias, qn, kn, bd)
    carried = [] if prev_kv is None else list(prev_kv)
    n_in = 1 + len(consts)
    return pl.pallas_call(
        _inproj_body,
        grid=(n // tm,),
        in_specs=[row(d)] + [_const_spec(a.shape) for a in consts]
        + [pl.BlockSpec(memory_space=pl.ANY)] * len(carried),
        out_specs=[row(c) for c in widths] + [kv_spec, kv_spec],
        out_shape=[jax.ShapeDtypeStruct((n, c), F32) for c in widths] + [kv_shape, kv_shape],
        input_output_aliases={n_in: len(widths), n_in + 1: len(widths) + 1} if carried else {},
        compiler_params=_params("parallel"),
        name="inproj",
    )(x, *consts, *carried)


def _shift_rows(x, halo, k):
    r = pltpu.roll(x, k, axis=0)
    hk = pltpu.roll(halo, k, axis=0)
    rid = lax.broadcasted_iota(jnp.int32, hk.shape, 0)
    head = jnp.where(rid < k, hk, r[:SUBLANES])
    return jnp.concatenate([head, r[SUBLANES:]], axis=0)


def _group_norm_store(dst_ref, y, nw):
    for g in range(SSD_GROUPS):
        sl = slice(g * SSD_GROUP_WIDTH, (g + 1) * SSD_GROUP_WIDTH)
        dst_ref[:, sl] = _rmsnorm(y[:, sl], nw[:, sl])


def _seqmix_body(z_ref, xbc_ref, xh_ref, dt_ref, sc_ref, sh_ref,
                 cw_ref, cb_ref, dtb_ref, alog_ref, dexp_ref, nw_ref, scw_ref, e_ref, tri_ref,
                 ya_ref, yb_ref, st_ref, cst_ref, scst_ref, hs_ref):
    t = pl.program_id(1)
    first = t == 0
    last = t == pl.num_programs(1) - 1
    L = xbc_ref.shape[0]

    @pl.when(first)
    def _():
        hs_ref[...] = jnp.zeros_like(hs_ref)

    xbc = xbc_ref[...]
    halo = jnp.where(first, 0.0, xh_ref[...])
    cw = cw_ref[...]
    conv = (_shift_rows(xbc, halo, 3) * cw[0:1] + _shift_rows(xbc, halo, 2) * cw[1:2]
            + _shift_rows(xbc, halo, 1) * cw[2:3] + xbc * cw[3:4])
    act = _silu(conv + cb_ref[...])
    xs = act[:, :SSD_WIDTH]

    dt = _softplus(dt_ref[...] + dtb_ref[...])
    la = dt * (-jnp.exp(alog_ref[...]))
    cs = _dot_split_left(tri_ref[...], la, 3)
    cs_t = cs.T
    tot = cs[L - 1:L, :]
    to_end = jnp.exp(tot - cs)
    stack = jnp.concatenate(
        [dt, jnp.exp(cs), to_end, jnp.broadcast_to(jnp.exp(tot), (SUBLANES, LANES))], axis=0)
    ex = _dot_split(stack, e_ref[...], 3)
    dt_e, ecs_e, te_e, cd_e = ex[:L], ex[L:2 * L], ex[2 * L:3 * L], ex[3 * L:3 * L + 1]
    xd = xs * dt_e
    xd_b = xd.astype(BF16)
    xdte_b = (xd * te_e).astype(BF16)

    rid = lax.broadcasted_iota(jnp.int32, (L, L), 0)
    cid = lax.broadcasted_iota(jnp.int32, (L, L), 1)
    causal = rid >= cid
    low_half = lax.broadcasted_iota(jnp.int32, (L, LANES), 1) < SSD_HEAD_DIM

    y_groups = []
    for g in range(SSD_GROUPS):
        gs = slice(g * SSD_GROUP_WIDTH, (g + 1) * SSD_GROUP_WIDTH)
        b_g = act[:, SSD_WIDTH + g * SSD_STATE:SSD_WIDTH + (g + 1) * SSD_STATE]
        c0 = SSD_WIDTH + SSD_GROUPS * SSD_STATE
        c_g = act[:, c0 + g * SSD_STATE:c0 + (g + 1) * SSD_STATE]
        b_gb, c_gb = b_g.astype(BF16), c_g.astype(BF16)
        cb = _dot_nt(c_gb, b_gb)
        h_in = hs_ref[g]
        y_off = _dot(c_gb, h_in.astype(BF16)) * ecs_e[:, gs]
        pieces = []
        for pr in range(HEADS_PER_GROUP // 2):
            ha = g * HEADS_PER_GROUP + 2 * pr
            xp = xd_b[:, ha * SSD_HEAD_DIM:(ha + 2) * SSD_HEAD_DIM]
            yd = []
            for hh in (ha, ha + 1):
                seg = cs[:, hh:hh + 1] - cs_t[hh:hh + 1, :]
                m = cb * jnp.exp(jnp.where(causal, seg, -jnp.inf))
                yd.append(_dot(m.astype(BF16), xp))
            pieces.append(jnp.where(low_half, yd[0], yd[1]))
        y_groups.append(jnp.concatenate(pieces, axis=1) + y_off)
        s_new = _dot(b_g.T.astype(BF16), xdte_b[:, gs])
        hs_ref[g] = h_in * cd_e[:, gs] + s_new

    y = jnp.concatenate(y_groups, axis=1) + xs * dexp_ref[...]
    y = y * _silu(z_ref[...])
    _group_norm_store(ya_ref, y, nw_ref[...])

    sc = sc_ref[...]
    p = sc[:, SC_WIDTH:2 * SC_WIDTH] * sc[:, 2 * SC_WIDTH:]
    sh = sh_ref[...]
    ph = jnp.where(first, 0.0, sh[:, SC_WIDTH:2 * SC_WIDTH] * sh[:, 2 * SC_WIDTH:])
    scw = scw_ref[...]
    yb_ref[...] = sc[:, :SC_WIDTH] * (
        _shift_rows(p, ph, 2) * scw[0:1] + _shift_rows(p, ph, 1) * scw[1:2] + p * scw[2:3])

    @pl.when(last)
    def _():
        for g in range(SSD_GROUPS):
            st_ref[g * HEADS_PER_GROUP:(g + 1) * HEADS_PER_GROUP] = hs_ref[g].T.reshape(
                HEADS_PER_GROUP, SSD_HEAD_DIM, SSD_STATE)
        cst_ref[...] = xbc[L - SUBLANES:]
        scst_ref[...] = p[L - SUBLANES:]


def _seqmix(z, xbc, dt, sc, cw, cb, dtb, alog, dexp, nw, scw, e_mat, tri):
    b, t, _ = z.shape
    L = SSD_CHUNK
    hb = L // SUBLANES

    def cur(c):
        return pl.BlockSpec((None, L, c), lambda i, j: (i, j, 0))

    def halo(c):
        return pl.BlockSpec((None, SUBLANES, c), lambda i, j: (i, jnp.maximum(j * hb - 1, 0), 0))

    def per_seq(shape):
        return pl.BlockSpec((None,) + shape, lambda i, j: (i,) + (0,) * len(shape))

    consts = (cw, cb, dtb, alog, dexp, nw, scw, e_mat, tri)
    return pl.pallas_call(
        _seqmix_body,
        grid=(b, t // L),
        in_specs=[cur(SSD_WIDTH), cur(SSD_CONV_DIM), halo(SSD_CONV_DIM), cur(LANES),
                  cur(3 * SC_WIDTH), halo(3 * SC_WIDTH)] + [_const_spec(a.shape) for a in consts],
        out_specs=[cur(SSD_WIDTH), cur(SC_WIDTH), per_seq((SSD_HEADS, SSD_HEAD_DIM, SSD_STATE)),
                   per_seq((SUBLANES, SSD_CONV_DIM)), per_seq((SUBLANES, SC_WIDTH))],
        out_shape=[jax.ShapeDtypeStruct((b, t, SSD_WIDTH), F32),
                   jax.ShapeDtypeStruct((b, t, SC_WIDTH), F32),
                   jax.ShapeDtypeStruct((b, SSD_HEADS, SSD_HEAD_DIM, SSD_STATE), F32),
                   jax.ShapeDtypeStruct((b, SUBLANES, SSD_CONV_DIM), F32),
                   jax.ShapeDtypeStruct((b, SUBLANES, SC_WIDTH), F32)],
        scratch_shapes=[pltpu.VMEM((SSD_GROUPS, SSD_STATE, SSD_GROUP_WIDTH), F32)],
        compiler_params=_params("parallel", "arbitrary"),
        name="seqmix",
    )(z, xbc, xbc, dt, sc, sc, *consts)


def _sb_body(bias_ref, q_ref, k_ref, v_ref, u_ref, o_ref):
    hp = pl.program_id(1)
    i = pl.program_id(2)
    tq = q_ref.shape[0]
    q2 = q_ref[...]
    low_half = lax.broadcasted_iota(jnp.int32, (tq, LANES), 1) < SB_HEAD_DIM
    q_heads = (jnp.where(low_half, q2, 0.0).astype(BF16), jnp.where(low_half, 0.0, q2).astype(BF16))
    biases = (bias_ref[2 * hp] * LOG2E, bias_ref[2 * hp + 1] * LOG2E)
    u = u_ref[...]

    def blocks(js, carry, mask):
        acc, cs = carry[0], list(carry[1:])
        kv = []
        for j in js:
            off = pl.multiple_of(j * tq, tq)
            kv.append((k_ref[:, :, pl.ds(off, tq)].reshape(LANES, tq).astype(BF16),
                       v_ref[:, :, pl.ds(off, tq)].reshape(LANES, tq).astype(BF16)))
        chains = [(n, h) for n in range(len(js)) for h in range(2)]
        z2 = [_dot(q_heads[h], kv[n][0]) * (SB_SCALE * LOG2E) + biases[h] for n, h in chains]
        sp2, lb2 = [], []
        for z in z2:
            sp = jnp.maximum(z, 0.0) + jnp.log2(1.0 + jnp.exp2(-jnp.abs(z)))
            lb2.append(z - sp)
            sp2.append(sp if mask is None else jnp.where(mask, sp, 0.0))
        tails = [_dot_split(sp, u, 2) for sp in sp2]
        ws = []
        for (n, h), lb, tail, sp in zip(chains, lb2, tails, sp2):
            w = jnp.exp2(lb - (tail + cs[h]))
            ws.append(w if mask is None else jnp.where(mask, w, 0.0))
            cs[h] = cs[h] + jnp.sum(sp, axis=1, keepdims=True)
        pv = [_dot_nt(w.astype(BF16), kv[n][1]) for (n, h), w in zip(chains, ws)]
        for n in range(len(js)):
            acc = acc + jnp.where(low_half, pv[2 * n], pv[2 * n + 1])
        return (acc, *cs)

    rid = lax.broadcasted_iota(jnp.int32, (tq, tq), 0)
    cid = lax.broadcasted_iota(jnp.int32, (tq, tq), 1)
    init = (jnp.zeros((tq, LANES), F32), jnp.zeros((tq, 1), F32), jnp.zeros((tq, 1), F32))
    carry = blocks([i], init, cid < rid)
    odd = i % 2
    carry = lax.cond(odd == 1, lambda cr: blocks([i - 1], cr, None), lambda cr: cr, carry)
    top = i - 1 - odd
    acc, _, _ = lax.fori_loop(
        0, i // 2, lambda p, cr: blocks([top - 2 * p, top - 2 * p - 1], cr, None), carry)
    o_ref[...] = acc


def _sb_attn(q, k_t, v_t, layer, bias, u):
    b, t, w = q.shape
    tq = SB_TQ
    qspec = pl.BlockSpec((None, tq, LANES), lambda bi, hp, i: (bi, i, hp))
    kvspec = pl.BlockSpec((None, None, 2, SB_HEAD_DIM, t), lambda bi, hp, i: (layer, bi, hp, 0, 0))
    return pl.pallas_call(
        _sb_body,
        grid=(b, w // LANES, t // tq),
        in_specs=[pl.BlockSpec(memory_space=pltpu.SMEM), qspec, kvspec, kvspec,
                  _const_spec(u.shape)],
        out_specs=qspec,
        out_shape=jax.ShapeDtypeStruct((b, t, w), F32),
        compiler_params=_params("parallel", "parallel", "arbitrary"),
        name="sb_attn",
    )(bias, q, k_t, v_t, u)


def _merge_body(x_ref, gates_ref, ya_ref, yb_ref, yc_ref, wa_ref, wb_ref, wc_ref, wo_ref, o_ref):
    gates = gates_ref[...]
    m = (gates[:, :D_MODEL] * _dot(ya_ref[...].astype(BF16), wa_ref[...])
         + gates[:, D_MODEL:2 * D_MODEL] * _dot(yb_ref[...].astype(BF16), wb_ref[...])
         + gates[:, 2 * D_MODEL:] * _dot(yc_ref[...].astype(BF16), wc_ref[...]))
    o_ref[...] = x_ref[...] + _dot(m.astype(BF16), wo_ref[...])


def _merge(x, gates, ya, yb, yc, wa, wb, wc, wo):
    n, d = x.shape
    tm = min(MERGE_TM, n)

    def row(c):
        return pl.BlockSpec((tm, c), lambda i: (i, 0))

    return pl.pallas_call(
        _merge_body,
        grid=(n // tm,),
        in_specs=[row(d), row(N_GATES), row(SSD_WIDTH), row(SC_WIDTH), row(SB_WIDTH)]
        + [_const_spec(a.shape) for a in (wa, wb, wc, wo)],
        out_specs=row(d),
        out_shape=jax.ShapeDtypeStruct((n, d), F32),
        compiler_params=_params("parallel"),
        name="merge",
    )(x, gates, ya, yb, yc, wa, wb, wc, wo)


def _dec_ssd_body(z_ref, xbc_ref, dt_ref, sc_ref, cst_ref, scst_ref, st_ref,
                  cw_ref, cb_ref, dtb_ref, alog_ref, dexp_ref, nw_ref, scw_ref, *rest):
    (ya_ref, yb_ref, ncst_ref, nscst_ref, nst_ref,
     xs_s, xdt_s, da_s, dab_s, bm_s, cm_s, xb_s, yt_s) = rest[-13:]
    h = pl.program_id(0)
    nb = z_ref.shape[0]

    @pl.when(h == 0)
    def _():
        xbc = xbc_ref[...]
        cw = cw_ref[...]
        conv = (cst_ref[0] * cw[0:1] + cst_ref[1] * cw[1:2] + cst_ref[2] * cw[2:3]
                + xbc * cw[3:4])
        ncst_ref[0] = cst_ref[1]
        ncst_ref[1] = cst_ref[2]
        ncst_ref[2] = xbc
        act = _silu(conv + cb_ref[...])
        xs = act[:, :SSD_WIDTH]
        xs_s[...] = xs
        c0 = SSD_WIDTH + SSD_GROUPS * SSD_STATE
        for g in range(SSD_GROUPS):
            bm_s[g] = act[:, SSD_WIDTH + g * SSD_STATE:SSD_WIDTH + (g + 1) * SSD_STATE]
            cm_s[g] = act[:, c0 + g * SSD_STATE:c0 + (g + 1) * SSD_STATE]
        dt = _softplus(dt_ref[...] + dtb_ref[...])
        da_s[...] = jnp.exp(dt * (-jnp.exp(alog_ref[...])))
        dt_t = dt.T
        xs_t = xs.T
        for hh in range(SSD_HEADS):
            xdt_s[hh] = xs_t[hh * SSD_HEAD_DIM:(hh + 1) * SSD_HEAD_DIM] * dt_t[hh:hh + 1, :]
        sc = sc_ref[...]
        p = sc[:, SC_WIDTH:2 * SC_WIDTH] * sc[:, 2 * SC_WIDTH:]
        scw = scw_ref[...]
        yb_ref[...] = sc[:, :SC_WIDTH] * (
            scst_ref[0] * scw[0:1] + scst_ref[1] * scw[1:2] + p * scw[2:3])
        nscst_ref[0] = scst_ref[1]
        nscst_ref[1] = p

    lane_t = lax.broadcasted_iota(jnp.int32, (nb, LANES), 1)
    da_col = jnp.sum(jnp.where(lane_t == h, da_s[...], 0.0), axis=1, keepdims=True)
    dab_s[...] = jnp.broadcast_to(da_col, (nb, LANES))

    xt = xdt_s[h]
    for b in range(nb):
        xb_s[b] = jnp.broadcast_to(xt[:, b:b + 1], (SSD_HEAD_DIM, LANES))

    g = h // HEADS_PER_GROUP
    lane = lax.broadcasted_iota(jnp.int32, (SSD_HEAD_DIM, LANES), 1)
    yacc = jnp.zeros((SSD_HEAD_DIM, LANES), F32)
    for b in range(nb):
        hn = st_ref[b] * dab_s[b:b + 1, :] + xb_s[b] * bm_s[g, b:b + 1, :]
        nst_ref[b] = hn
        ycol = jnp.sum(hn * cm_s[g, b:b + 1, :], axis=1, keepdims=True)
        yacc = jnp.where(lane == b, ycol, yacc)
    yt_s[h] = yacc

    @pl.when(h == pl.num_programs(0) - 1)
    def _():
        xs = xs_s[...]
        y = yt_s[...].reshape(SSD_WIDTH, nb).T + xs * dexp_ref[...]
        y = y * _silu(z_ref[...])
        _group_norm_store(ya_ref, y, nw_ref[...])


def _dec_ssd(z, xbc, dt, sc, cst, scst, state, layer, prev_state, cw, cb, dtb, alog, dexp, nw, scw):
    nb = z.shape[0]
    consts = (cw, cb, dtb, alog, dexp, nw, scw)
    st_spec = pl.BlockSpec((None, nb, None, SSD_HEAD_DIM, SSD_STATE), lambda h: (layer, 0, h, 0, 0))
    carried = [] if prev_state is None else [prev_state]
    n_in = 7 + len(consts)
    return pl.pallas_call(
        _dec_ssd_body,
        grid=(SSD_HEADS,),
        in_specs=[_const_spec(a.shape) for a in (z, xbc, dt, sc, cst, scst)] + [st_spec]
        + [_const_spec(a.shape) for a in consts] + [pl.BlockSpec(memory_space=pl.ANY)] * len(carried),
        out_specs=[pl.BlockSpec((nb, SSD_WIDTH), lambda h: (0, 0)),
                   pl.BlockSpec((nb, SC_WIDTH), lambda h: (0, 0)),
                   pl.BlockSpec(cst.shape, lambda h: (0, 0, 0)),
                   pl.BlockSpec(scst.shape, lambda h: (0, 0, 0)),
                   st_spec],
        out_shape=[jax.ShapeDtypeStruct((nb, SSD_WIDTH), F32),
                   jax.ShapeDtypeStruct((nb, SC_WIDTH), F32),
                   jax.ShapeDtypeStruct(cst.shape, F32),
                   jax.ShapeDtypeStruct(scst.shape, F32),
                   jax.ShapeDtypeStruct(state.shape, F32)],
        scratch_shapes=[pltpu.VMEM((nb, SSD_WIDTH), F32),
                        pltpu.VMEM((SSD_HEADS, SSD_HEAD_DIM, nb), F32),
                        pltpu.VMEM((nb, LANES), F32),
                        pltpu.VMEM((nb, LANES), F32),
                        pltpu.VMEM((SSD_GROUPS, nb, SSD_STATE), F32),
                        pltpu.VMEM((SSD_GROUPS, nb, SSD_STATE), F32),
                        pltpu.VMEM((nb, SSD_HEAD_DIM, LANES), F32),
                        pltpu.VMEM((SSD_HEADS, SSD_HEAD_DIM, nb), F32)],
        input_output_aliases={n_in: 4} if carried else {},
        compiler_params=_params("arbitrary"),
        name="dec_ssd",
    )(z, xbc, dt, sc, cst, scst, state, *consts, *carried)


def _dec_attn_body(pt_ref, qt_ref, bias_ref, u_ref, p_ref, *refs):
    n_pages = (len(refs) - 2) // 2
    k_refs, v_refs = refs[:n_pages], refs[n_pages:2 * n_pages]
    o_ref, w_s = refs[2 * n_pages], refs[2 * n_pages + 1]
    b = pl.program_id(0)
    nb = qt_ref.shape[1]

    @pl.when(b == 0)
    def _():
        o_ref[...] = jnp.zeros_like(o_ref)

    token = lax.broadcasted_iota(jnp.int32, (SB_WIDTH, nb), 1) == b
    q_col = jnp.sum(jnp.where(token, qt_ref[...], 0.0), axis=1, keepdims=True)
    sub = lax.broadcasted_iota(jnp.int32, (SB_HEADS, LANES), 0)
    z_pages = []
    for p in range(n_pages):
        zp = jnp.zeros((SB_HEADS, LANES), F32)
        for h in range(SB_HEADS):
            prod = k_refs[p][h] * q_col[h * SB_HEAD_DIM:(h + 1) * SB_HEAD_DIM]
            zp = jnp.where(sub == h, jnp.sum(prod, axis=0, keepdims=True), zp)
        z_pages.append(zp)
    z = jnp.concatenate(z_pages, axis=0) * SB_SCALE + bias_ref[...]
    sp = _softplus(z)
    lk = -sp
    n = z.shape[0]
    tot = jnp.broadcast_to(jnp.sum(lk, axis=1, keepdims=True), (n, LANES))
    tail = _dot_split(lk, u_ref[...], 3) + _dot_split_left(p_ref[...], tot, 3)
    w_s[...] = jnp.exp((z - sp) + tail)
    cols = []
    for h in range(SB_HEADS):
        acc = jnp.zeros((SB_HEAD_DIM, LANES), F32)
        for p in range(n_pages):
            r = p * SB_HEADS + h
            acc = acc + v_refs[p][h] * w_s[r:r + 1, :]
        cols.append(jnp.sum(acc, axis=1, keepdims=True))
    o_col = jnp.concatenate(cols, axis=0)
    o_ref[...] = jnp.where(token, o_col, o_ref[...])


def _dec_attn(page_table, q_t, cache_kt, cache_vt, layer, bias_col, u, pmat):
    nb, n_pages = page_table.shape
    page = cache_kt.shape[-1]

    def page_spec(j):
        return pl.BlockSpec((None, None, SB_HEADS, SB_HEAD_DIM, page),
                            lambda b, pt: (layer, pt[b, j], 0, 0, 0))

    def const(a):
        return pl.BlockSpec(a.shape, lambda b, pt: (0,) * a.ndim, pipeline_mode=pl.Buffered(1))

    grid_spec = pltpu.PrefetchScalarGridSpec(
        num_scalar_prefetch=1,
        grid=(nb,),
        in_specs=[const(q_t), const(bias_col), const(u), const(pmat)]
        + [page_spec(j) for j in range(n_pages)] * 2,
        out_specs=pl.BlockSpec((SB_WIDTH, nb), lambda b, pt: (0, 0)),
        scratch_shapes=[pltpu.VMEM((n_pages * SB_HEADS, LANES), F32)],
    )
    return pl.pallas_call(
        _dec_attn_body,
        grid_spec=grid_spec,
        out_shape=jax.ShapeDtypeStruct((SB_WIDTH, nb), F32),
        compiler_params=_params("arbitrary"),
        name="dec_attn",
    )(page_table, q_t, bias_col, u, pmat, *([cache_kt] * n_pages), *([cache_vt] * n_pages))


def _tri01(n, strict_upper_rows):
    r = lax.broadcasted_iota(jnp.int32, (n, n), 0)
    c = lax.broadcasted_iota(jnp.int32, (n, n), 1)
    return ((r > c) if strict_upper_rows else (r >= c)).astype(BF16)


def kernel(x_prompt, x_sample, cache_sb_k, cache_sb_v, page_table, state_ssd, state_ssd_conv, state_sc_conv, ffn1_norm, ffn1_w_gate, ffn1_w_up, ffn1_w_down, mix_norm, w_in, gate_bias, ssd_conv_w, ssd_conv_b, ssd_dt_bias, ssd_a_log, ssd_d, ssd_norm, sc_conv_w, sb_q_norm, sb_k_norm, sb_logit_bias, w_proj_a, w_proj_b, w_proj_c, w_out, ffn2_norm, ffn2_w_gate, ffn2_w_up, ffn2_w_down):
    depth = w_in.shape[0]
    bsz, seq, d = x_prompt.shape
    dec_b = x_sample.shape[0]
    page = cache_sb_k.shape[2]
    n_pages = page_table.shape[1]
    cache_kt = cache_sb_k.transpose(0, 1, 3, 4, 2)
    cache_vt = cache_sb_v.transpose(0, 1, 3, 4, 2)

    tri_incl = _tri01(SSD_CHUNK, False)
    u_tq = _tri01(SB_TQ, True)
    u_page = _tri01(page, True)
    lane = jnp.arange(LANES)
    bd = ((lane[:, None] // SB_HEAD_DIM) == (lane[None, :] // SB_HEAD_DIM)).astype(BF16)
    col = jnp.arange(SSD_WIDTH)
    e_mat = (lane[:, None] == (col[None, :] // SSD_HEAD_DIM)).astype(BF16)
    r = jnp.arange(n_pages * SB_HEADS)
    pmat = ((r[:, None] % SB_HEADS == r[None, :] % SB_HEADS)
            & (r[None, :] // SB_HEADS > r[:, None] // SB_HEADS)).astype(BF16)

    def pad_lanes(a):
        return jnp.pad(a, ((0, 0), (0, LANES - a.shape[1])))

    xp = x_prompt.reshape(bsz * seq, d)
    xs = x_sample.reshape(dec_b * x_sample.shape[1], d)
    kv_p = kv_s = nst = None
    hst, cst, scst, ncst, nscst = [], [], [], [], []
    for l in range(depth):
        wl = w_in[l]
        c_dt = N_GATES + SSD_WIDTH + SSD_CONV_DIM
        c_q = c_dt + SSD_HEADS + 3 * SC_WIDTH
        w_packed = jnp.concatenate(
            [wl[:, :c_dt], wl[:, c_dt + SSD_HEADS:c_q + SB_WIDTH],
             pad_lanes(wl[:, c_dt:c_dt + SSD_HEADS])], axis=1).astype(BF16)
        wkv_t = wl[:, c_q + SB_WIDTH:].T.astype(BF16)
        f1 = (ffn1_norm[l][None], ffn1_w_gate[l].astype(BF16), ffn1_w_up[l].astype(BF16),
              ffn1_w_down[l].astype(BF16))
        f2 = (ffn2_norm[l][None], ffn2_w_gate[l].astype(BF16), ffn2_w_up[l].astype(BF16),
              ffn2_w_down[l].astype(BF16))
        ip = (mix_norm[l][None], w_packed, wkv_t, gate_bias[l][None],
              jnp.tile(sb_q_norm[l], SB_HEADS)[None], sb_k_norm[l][:, None], bd)
        ssd_p = (ssd_conv_w[l], ssd_conv_b[l][None], pad_lanes(ssd_dt_bias[l][None]),
                 pad_lanes(ssd_a_log[l][None]), jnp.repeat(ssd_d[l], SSD_HEAD_DIM)[None],
                 ssd_norm[l][None], sc_conv_w[l])
        mg = (w_proj_a[l].astype(BF16), w_proj_b[l].astype(BF16), w_proj_c[l].astype(BF16),
              w_out[l].astype(BF16))

        x1 = _ffn(xp, *f1)
        gates, z, xbc, dt, sc, q, *kv_p = _inproj(x1, seq, depth, l, kv_p, *ip)
        as_seq = lambda a: a.reshape(bsz, seq, a.shape[-1])
        ya, yb, h_l, cst8, scst8 = _seqmix(as_seq(z), as_seq(xbc), as_seq(dt), as_seq(sc),
                                            *ssd_p, e_mat, tri_incl)
        yc = _sb_attn(as_seq(q), *kv_p, l, sb_logit_bias[l], u_tq)
        x2 = _merge(x1, gates, ya.reshape(-1, SSD_WIDTH), yb.reshape(-1, SC_WIDTH),
                    yc.reshape(-1, SB_WIDTH), *mg)
        xp = _ffn(x2, *f2)
        hst.append(h_l)
        cst.append(cst8[:, SUBLANES - (SSD_CONV - 1):])
        scst.append(scst8[:, SUBLANES - (SC_CONV - 1):])

        s1 = _ffn(xs, *f1)
        gates, z, xbc, dt, sc, q, *kv_s = _inproj(s1, dec_b, depth, l, kv_s, *ip)
        ya, yb, ncst_l, nscst_l, nst = _dec_ssd(
            z, xbc, dt, sc, state_ssd_conv[l].transpose(1, 0, 2), state_sc_conv[l].transpose(1, 0, 2),
            state_ssd, l, nst, *ssd_p)
        yc_t = _dec_attn(page_table, q.T, cache_kt, cache_vt, l,
                         jnp.tile(sb_logit_bias[l], n_pages)[:, None], u_page, pmat)
        s2 = _merge(s1, gates, ya, yb, yc_t.T, *mg)
        xs = _ffn(s2, *f2)
        ncst.append(ncst_l.transpose(1, 0, 2))
        nscst.append(nscst_l.transpose(1, 0, 2))

    kp, vp = (a.transpose(0, 1, 4, 2, 3) for a in kv_p)
    ks, vs = (a.transpose(0, 4, 1, 2, 3) for a in kv_s)
    return (xp.reshape(bsz, seq, d), xs.reshape(dec_b, x_sample.shape[1], d),
            kp, vp, ks, vs, jnp.stack(hst), nst, jnp.stack(cst), jnp.stack(ncst),
            jnp.stack(scst), jnp.stack(nscst))
```

```python
import functools

import jax
import jax.numpy as jnp
from jax import lax
from jax.experimental import pallas as pl
from jax.experimental.pallas import tpu as pltpu

F32 = jnp.float32
BF16 = jnp.bfloat16
EPS = 1e-6

LANES = 128
SUBLANES = 8
VMEM_LIMIT = 56 * 1024 * 1024

D_MODEL = 1024
SSD_HEADS = 16
SSD_HEAD_DIM = 64
SSD_WIDTH = SSD_HEADS * SSD_HEAD_DIM
SSD_GROUPS = 2
SSD_STATE = 128
SSD_CONV = 4
SSD_CONV_DIM = SSD_WIDTH + 2 * SSD_GROUPS * SSD_STATE
SSD_GROUP_WIDTH = SSD_WIDTH // SSD_GROUPS
HEADS_PER_GROUP = SSD_HEADS // SSD_GROUPS
SC_WIDTH = 512
SC_CONV = 3
SB_HEADS = 8
SB_HEAD_DIM = 64
SB_WIDTH = SB_HEADS * SB_HEAD_DIM
SB_SCALE = SB_HEAD_DIM ** -0.5
LOG2E = 1.4426950408889634
N_GATES = 3 * D_MODEL

C_GATE = 0
C_Z = C_GATE + N_GATES
C_XBC = C_Z + SSD_WIDTH
C_SC = C_XBC + SSD_CONV_DIM
C_Q = C_SC + 3 * SC_WIDTH
C_DT = C_Q + SB_WIDTH
C_END = C_DT + LANES

FFN_TM = 512
FFN_CHUNKS = 2
INPROJ_TM = 256
MERGE_TM = 256
SSD_CHUNK = 128
SB_TQ = 256


def _dot(a, b):
    return jnp.dot(a, b, preferred_element_type=F32)


def _dot_nt(a, b):
    return lax.dot_general(a, b, (((1,), (1,)), ((), ())), preferred_element_type=F32)


def _bf16_pieces(a, n):
    pieces, r = [], a
    for i in range(n):
        p = r.astype(BF16)
        pieces.append(p)
        if i + 1 < n:
            r = r - p.astype(F32)
    return pieces


def _dot_split(a, b01, n):
    out = None
    for p in _bf16_pieces(a, n):
        d = _dot(p, b01)
        out = d if out is None else out + d
    return out


def _dot_split_left(a01, b, n):
    out = None
    for p in _bf16_pieces(b, n):
        d = _dot(a01, p)
        out = d if out is None else out + d
    return out


def _rmsnorm(x, g):
    return x * lax.rsqrt(jnp.mean(x * x, axis=-1, keepdims=True) + EPS) * g


def _silu(x):
    return x * jax.nn.sigmoid(x)


def _softplus(x):
    return jnp.maximum(x, 0.0) + jnp.log1p(jnp.exp(-jnp.abs(x)))


def _const_spec(shape):
    nd = len(shape)
    return pl.BlockSpec(shape, lambda *_: (0,) * nd, pipeline_mode=pl.Buffered(1))


def _params(*sem):
    return pltpu.CompilerParams(dimension_semantics=sem, vmem_limit_bytes=VMEM_LIMIT)


def _ffn_value(x, g_ref, wg_ref, wu_ref, wd_ref):
    h = _rmsnorm(x, g_ref[...]).astype(BF16)
    c = wg_ref.shape[1] // FFN_CHUNKS
    y = None
    for i in range(FFN_CHUNKS):
        a = _dot(h, wg_ref[:, i * c:(i + 1) * c])
        u = _dot(h, wu_ref[:, i * c:(i + 1) * c])
        act = (_silu(a) * u).astype(BF16)
        yi = _dot(act, wd_ref[i * c:(i + 1) * c, :])
        y = yi if y is None else y + yi
    return x + 0.5 * y


def _ffn_body(x_ref, g_ref, wg_ref, wu_ref, wd_ref, o_ref):
    o_ref[...] = _ffn_value(x_ref[...], g_ref, wg_ref, wu_ref, wd_ref)


def _ffn(x, g, wg, wu, wd):
    n, d = x.shape
    tm = min(FFN_TM, n)
    row = pl.BlockSpec((tm, d), lambda i: (i, 0))
    return pl.pallas_call(
        _ffn_body,
        grid=(n // tm,),
        in_specs=[row, _const_spec(g.shape), _const_spec(wg.shape), _const_spec(wu.shape),
                  _const_spec(wd.shape)],
        out_specs=row,
        out_shape=jax.ShapeDtypeStruct((n, d), F32),
        compiler_params=_params("parallel"),
        name="ffn",
    )(x, g, wg, wu, wd)


def _inproj_body(x_ref, g_ref, w_ref, wkv_ref, gb_ref, qn_ref, kn_ref, bd_ref, *rest):
    gates_ref, z_ref, xbc_ref, dt_ref, sc_ref, q_ref, kt_ref, vt_ref = rest[-8:]
    h = _rmsnorm(x_ref[...], g_ref[...]).astype(BF16)

    def seg(lo, hi):
        return _dot(h, w_ref[:, lo:hi])

    gates_ref[...] = jax.nn.sigmoid(seg(C_GATE, C_Z) + gb_ref[...])
    z_ref[...] = seg(C_Z, C_XBC)
    xbc_ref[...] = seg(C_XBC, C_SC)
    sc_ref[...] = seg(C_SC, C_Q)
    dt_ref[...] = seg(C_DT, C_END)

    q = seg(C_Q, C_DT)
    bd = bd_ref[...]
    qn = qn_ref[...]
    for s in range(SB_WIDTH // LANES):
        sl = slice(s * LANES, (s + 1) * LANES)
        qs = q[:, sl]
        ms = _dot_split(qs * qs, bd, 2) * (1.0 / SB_HEAD_DIM)
        q_ref[:, sl] = qs * lax.rsqrt(ms + EPS) * qn[:, sl]

    kv_t = _dot_nt(wkv_ref[...], h)
    tm = kv_t.shape[1]
    k_t = kv_t[:SB_WIDTH].reshape(SB_HEADS, SB_HEAD_DIM, tm)
    ms = jnp.mean(k_t * k_t, axis=1, keepdims=True)
    kt_ref[...] = k_t * lax.rsqrt(ms + EPS) * kn_ref[...]
    vt_ref[...] = kv_t[SB_WIDTH:].reshape(SB_HEADS, SB_HEAD_DIM, tm)


def _inproj(x, seq, depth, layer, prev_kv, g, w, wkv_t, gate_bias, qn, kn, bd):
    n, d = x.shape
    tm = min(INPROJ_TM, seq)
    nt = seq // tm

    def row(c):
        return pl.BlockSpec((tm, c), lambda i: (i, 0))

    kv_spec = pl.BlockSpec((None, None, SB_HEADS, SB_HEAD_DIM, tm),
                           lambda i: (layer, i // nt, 0, 0, i % nt))
    kv_shape = jax.ShapeDtypeStruct((depth, n // seq, SB_HEADS, SB_HEAD_DIM, seq), F32)
    widths = (N_GATES, SSD_WIDTH, SSD_CONV_DIM, LANES, 3 * SC_WIDTH, SB_WIDTH)
    consts = (g, w, wkv_t, gate_bias, qn, kn, bd)
    carried = [] if prev_kv is None else list(prev_kv)
    n_in = 1 + len(consts)
    return pl.pallas_call(
        _inproj_body,
        grid=(n // tm,),
        in_specs=[row(d)] + [_const_spec(a.shape) for a in consts]
        + [pl.BlockSpec(memory_space=pl.ANY)] * len(carried),
        out_specs=[row(c) for c in widths] + [kv_spec, kv_spec],
        out_shape=[jax.ShapeDtypeStruct((n, c), F32) for c in widths] + [kv_shape, kv_shape],
        input_output_aliases={n_in: len(widths), n_in + 1: len(widths) + 1} if carried else {},
        compiler_params=_params("parallel"),
        name="inproj",
    )(x, *consts, *carried)


def _shift_rows(x, halo, k):
    r = pltpu.roll(x, k, axis=0)
    hk = pltpu.roll(halo, k, axis=0)
    rid = lax.broadcasted_iota(jnp.int32, hk.shape, 0)
    head = jnp.where(rid < k, hk, r[:SUBLANES])
    return jnp.concatenate([head, r[SUBLANES:]], axis=0)


def _group_norm_store(dst_ref, y, nw):
    for g in range(SSD_GROUPS):
        sl = slice(g * SSD_GROUP_WIDTH, (g + 1) * SSD_GROUP_WIDTH)
        dst_ref[:, sl] = _rmsnorm(y[:, sl], nw[:, sl])


def _seqmix_body(z_ref, xbc_ref, xh_ref, dt_ref, sc_ref, sh_ref,
                 cw_ref, cb_ref, dtb_ref, alog_ref, dexp_ref, nw_ref, scw_ref, e_ref, tri_ref,
                 ya_ref, yb_ref, st_ref, cst_ref, scst_ref, hs_ref):
    t = pl.program_id(1)
    first = t == 0
    last = t == pl.num_programs(1) - 1
    L = xbc_ref.shape[0]

    @pl.when(first)
    def _():
        hs_ref[...] = jnp.zeros_like(hs_ref)

    xbc = xbc_ref[...]
    halo = jnp.where(first, 0.0, xh_ref[...])
    cw = cw_ref[...]
    conv = (_shift_rows(xbc, halo, 3) * cw[0:1] + _shift_rows(xbc, halo, 2) * cw[1:2]
            + _shift_rows(xbc, halo, 1) * cw[2:3] + xbc * cw[3:4])
    act = _silu(conv + cb_ref[...])
    xs = act[:, :SSD_WIDTH]

    dt = _softplus(dt_ref[...] + dtb_ref[...])
    la = dt * (-jnp.exp(alog_ref[...]))
    cs = _dot_split_left(tri_ref[...], la, 3)
    cs_t = cs.T
    tot = cs[L - 1:L, :]
    to_end = jnp.exp(tot - cs)
    stack = jnp.concatenate(
        [dt, jnp.exp(cs), to_end, jnp.broadcast_to(jnp.exp(tot), (SUBLANES, LANES))], axis=0)
    ex = _dot_split(stack, e_ref[...], 3)
    dt_e, ecs_e, te_e, cd_e = ex[:L], ex[L:2 * L], ex[2 * L:3 * L], ex[3 * L:3 * L + 1]
    xd = xs * dt_e
    xd_b = xd.astype(BF16)
    xdte_b = (xd * te_e).astype(BF16)

    rid = lax.broadcasted_iota(jnp.int32, (L, L), 0)
    cid = lax.broadcasted_iota(jnp.int32, (L, L), 1)
    causal = rid >= cid
    low_half = lax.broadcasted_iota(jnp.int32, (L, LANES), 1) < SSD_HEAD_DIM

    y_groups = []
    for g in range(SSD_GROUPS):
        gs = slice(g * SSD_GROUP_WIDTH, (g + 1) * SSD_GROUP_WIDTH)
        b_g = act[:, SSD_WIDTH + g * SSD_STATE:SSD_WIDTH + (g + 1) * SSD_STATE]
        c0 = SSD_WIDTH + SSD_GROUPS * SSD_STATE
        c_g = act[:, c0 + g * SSD_STATE:c0 + (g + 1) * SSD_STATE]
        b_gb, c_gb = b_g.astype(BF16), c_g.astype(BF16)
        cb = _dot_nt(c_gb, b_gb)
        h_in = hs_ref[g]
        y_off = _dot(c_gb, h_in.astype(BF16)) * ecs_e[:, gs]
        pieces = []
        for pr in range(HEADS_PER_GROUP // 2):
            ha = g * HEADS_PER_GROUP + 2 * pr
            xp = xd_b[:, ha * SSD_HEAD_DIM:(ha + 2) * SSD_HEAD_DIM]
            yd = []
            for hh in (ha, ha + 1):
                seg = cs[:, hh:hh + 1] - cs_t[hh:hh + 1, :]
                m = cb * jnp.exp(jnp.where(causal, seg, -jnp.inf))
                yd.append(_dot(m.astype(BF16), xp))
            pieces.append(jnp.where(low_half, yd[0], yd[1]))
        y_groups.append(jnp.concatenate(pieces, axis=1) + y_off)
        s_new = _dot(b_g.T.astype(BF16), xdte_b[:, gs])
        hs_ref[g] = h_in * cd_e[:, gs] + s_new

    y = jnp.concatenate(y_groups, axis=1) + xs * dexp_ref[...]
    y = y * _silu(z_ref[...])
    _group_norm_store(ya_ref, y, nw_ref[...])

    sc = sc_ref[...]
    p = sc[:, SC_WIDTH:2 * SC_WIDTH] * sc[:, 2 * SC_WIDTH:]
    sh = sh_ref[...]
    ph = jnp.where(first, 0.0, sh[:, SC_WIDTH:2 * SC_WIDTH] * sh[:, 2 * SC_WIDTH:])
    scw = scw_ref[...]
    yb_ref[...] = sc[:, :SC_WIDTH] * (
        _shift_rows(p, ph, 2) * scw[0:1] + _shift_rows(p, ph, 1) * scw[1:2] + p * scw[2:3])

    @pl.when(last)
    def _():
        for g in range(SSD_GROUPS):
            st_ref[g * HEADS_PER_GROUP:(g + 1) * HEADS_PER_GROUP] = hs_ref[g].T.reshape(
                HEADS_PER_GROUP, SSD_HEAD_DIM, SSD_STATE)
        cst_ref[...] = xbc[L - SUBLANES:]
        scst_ref[...] = p[L - SUBLANES:]


def _seqmix(z, xbc, dt, sc, cw, cb, dtb, alog, dexp, nw, scw, e_mat, tri):
    b, t, _ = z.shape
    L = SSD_CHUNK
    hb = L // SUBLANES

    def cur(c):
        return pl.BlockSpec((None, L, c), lambda i, j: (i, j, 0))

    def halo(c):
        return pl.BlockSpec((None, SUBLANES, c), lambda i, j: (i, jnp.maximum(j * hb - 1, 0), 0))

    def per_seq(shape):
        return pl.BlockSpec((None,) + shape, lambda i, j: (i,) + (0,) * len(shape))

    consts = (cw, cb, dtb, alog, dexp, nw, scw, e_mat, tri)
    return pl.pallas_call(
        _seqmix_body,
        grid=(b, t // L),
        in_specs=[cur(SSD_WIDTH), cur(SSD_CONV_DIM), halo(SSD_CONV_DIM), cur(LANES),
                  cur(3 * SC_WIDTH), halo(3 * SC_WIDTH)] + [_const_spec(a.shape) for a in consts],
        out_specs=[cur(SSD_WIDTH), cur(SC_WIDTH), per_seq((SSD_HEADS, SSD_HEAD_DIM, SSD_STATE)),
                   per_seq((SUBLANES, SSD_CONV_DIM)), per_seq((SUBLANES, SC_WIDTH))],
        out_shape=[jax.ShapeDtypeStruct((b, t, SSD_WIDTH), F32),
                   jax.ShapeDtypeStruct((b, t, SC_WIDTH), F32),
                   jax.ShapeDtypeStruct((b, SSD_HEADS, SSD_HEAD_DIM, SSD_STATE), F32),
                   jax.ShapeDtypeStruct((b, SUBLANES, SSD_CONV_DIM), F32),
                   jax.ShapeDtypeStruct((b, SUBLANES, SC_WIDTH), F32)],
        scratch_shapes=[pltpu.VMEM((SSD_GROUPS, SSD_STATE, SSD_GROUP_WIDTH), F32)],
        compiler_params=_params("parallel", "arbitrary"),
        name="seqmix",
    )(z, xbc, xbc, dt, sc, sc, *consts)


def _sb_body(bias_ref, q_ref, k_ref, v_ref, u_ref, o_ref):
    hp = pl.program_id(1)
    i = pl.program_id(2)
    tq = q_ref.shape[0]
    q2 = q_ref[...]
    low_half = lax.broadcasted_iota(jnp.int32, (tq, LANES), 1) < SB_HEAD_DIM
    q_heads = (jnp.where(low_half, q2, 0.0).astype(BF16), jnp.where(low_half, 0.0, q2).astype(BF16))
    biases = (bias_ref[2 * hp] * LOG2E, bias_ref[2 * hp + 1] * LOG2E)
    u = u_ref[...]

    def blocks(js, carry, masks):
        acc, cs = carry[0], list(carry[1:])
        kv = []
        for j in js:
            off = pl.multiple_of(j * tq, tq)
            kv.append((k_ref[:, :, pl.ds(off, tq)].reshape(LANES, tq).astype(BF16),
                       v_ref[:, :, pl.ds(off, tq)].reshape(LANES, tq).astype(BF16)))
        chains = [(n, h) for n in range(len(js)) for h in range(2)]
        z2 = [_dot(q_heads[h], kv[n][0]) * (SB_SCALE * LOG2E) + biases[h] for n, h in chains]
        sp2, lb2 = [], []
        for (n, h), z in zip(chains, z2):
            neg_abs = lax.bitcast_convert_type(
                lax.bitcast_convert_type(z, jnp.uint32) | jnp.uint32(0x80000000), F32)
            sp = jnp.maximum(z, 0.0) + jnp.log2(1.0 + jnp.exp2(neg_abs))
            lb2.append(z - sp)
            sp2.append(sp if masks[n] is None else jnp.where(masks[n], sp, 0.0))
        tails = [_dot_split(sp, u, 2) for sp in sp2]
        ws = []
        for (n, h), lb, tail, sp in zip(chains, lb2, tails, sp2):
            w = jnp.exp2(lb - (tail + cs[h]))
            ws.append(w if masks[n] is None else jnp.where(masks[n], w, 0.0))
            cs[h] = cs[h] + jnp.sum(sp, axis=1, keepdims=True)
        pv = [_dot_nt(w.astype(BF16), kv[n][1]) for (n, h), w in zip(chains, ws)]
        for n in range(len(js)):
            acc = acc + jnp.where(low_half, pv[2 * n], pv[2 * n + 1])
        return (acc, *cs)

    rid = lax.broadcasted_iota(jnp.int32, (tq, tq), 0)
    cid = lax.broadcasted_iota(jnp.int32, (tq, tq), 1)
    diag = cid < rid
    init = (jnp.zeros((tq, LANES), F32), jnp.zeros((tq, 1), F32), jnp.zeros((tq, 1), F32))
    carry = lax.cond(i == 0, lambda cr: blocks([i], cr, [diag]),
                     lambda cr: blocks([i, i - 1], cr, [diag, None]), init)
    rest = jnp.maximum(i - 1, 0)
    odd = rest % 2
    carry = lax.cond(odd == 1, lambda cr: blocks([i - 2], cr, [None]), lambda cr: cr, carry)
    top = i - 2 - odd
    acc, _, _ = lax.fori_loop(
        0, rest // 2,
        lambda p, cr: blocks([top - 2 * p, top - 2 * p - 1], cr, [None, None]), carry)
    o_ref[...] = acc


def _sb_attn(q, k_t, v_t, layer, bias, u):
    b, t, w = q.shape
    tq = SB_TQ
    qspec = pl.BlockSpec((None, tq, LANES), lambda bi, hp, i: (bi, i, hp))
    kvspec = pl.BlockSpec((None, None, 2, SB_HEAD_DIM, t), lambda bi, hp, i: (layer, bi, hp, 0, 0))
    return pl.pallas_call(
        _sb_body,
        grid=(b, w // LANES, t // tq),
        in_specs=[pl.BlockSpec(memory_space=pltpu.SMEM), qspec, kvspec, kvspec,
                  _const_spec(u.shape)],
        out_specs=qspec,
        out_shape=jax.ShapeDtypeStruct((b, t, w), F32),
        compiler_params=_params("parallel", "parallel", "arbitrary"),
        name="sb_attn",
    )(bias, q, k_t, v_t, u)


def _merge_ffn_body(x_ref, gates_ref, ya_ref, yb_ref, yc_ref, wa_ref, wb_ref, wc_ref, wo_ref,
                    g_ref, wg_ref, wu_ref, wd_ref, o_ref):
    gates = gates_ref[...]
    m = (gates[:, :D_MODEL] * _dot(ya_ref[...].astype(BF16), wa_ref[...])
         + gates[:, D_MODEL:2 * D_MODEL] * _dot(yb_ref[...].astype(BF16), wb_ref[...])
         + gates[:, 2 * D_MODEL:] * _dot(yc_ref[...].astype(BF16), wc_ref[...]))
    x2 = x_ref[...] + _dot(m.astype(BF16), wo_ref[...])
    o_ref[...] = _ffn_value(x2, g_ref, wg_ref, wu_ref, wd_ref)


def _merge_ffn(x, gates, ya, yb, yc, wa, wb, wc, wo, g, wg, wu, wd):
    n, d = x.shape
    tm = min(MERGE_TM, n)

    def row(c):
        return pl.BlockSpec((tm, c), lambda i: (i, 0))

    consts = (wa, wb, wc, wo, g, wg, wu, wd)
    return pl.pallas_call(
        _merge_ffn_body,
        grid=(n // tm,),
        in_specs=[row(d), row(N_GATES), row(SSD_WIDTH), row(SC_WIDTH), row(SB_WIDTH)]
        + [_const_spec(a.shape) for a in consts],
        out_specs=row(d),
        out_shape=jax.ShapeDtypeStruct((n, d), F32),
        compiler_params=_params("parallel"),
        name="merge_ffn",
    )(x, gates, ya, yb, yc, *consts)


def _dec_ssd_body(z_ref, xbc_ref, dt_ref, sc_ref, cst_ref, scst_ref, st_ref,
                  cw_ref, cb_ref, dtb_ref, alog_ref, dexp_ref, nw_ref, scw_ref, *rest):
    (ya_ref, yb_ref, ncst_ref, nscst_ref, nst_ref,
     xs_s, xdt_s, da_s, dab_s, bm_s, cm_s, xb_s, yt_s) = rest[-13:]
    h = pl.program_id(0)
    nb = z_ref.shape[0]

    @pl.when(h == 0)
    def _():
        xbc = xbc_ref[...]
        cw = cw_ref[...]
        conv = (cst_ref[0] * cw[0:1] + cst_ref[1] * cw[1:2] + cst_ref[2] * cw[2:3]
                + xbc * cw[3:4])
        ncst_ref[0] = cst_ref[1]
        ncst_ref[1] = cst_ref[2]
        ncst_ref[2] = xbc
        act = _silu(conv + cb_ref[...])
        xs = act[:, :SSD_WIDTH]
        xs_s[...] = xs
        c0 = SSD_WIDTH + SSD_GROUPS * SSD_STATE
        for g in range(SSD_GROUPS):
            bm_s[g] = act[:, SSD_WIDTH + g * SSD_STATE:SSD_WIDTH + (g + 1) * SSD_STATE]
            cm_s[g] = act[:, c0 + g * SSD_STATE:c0 + (g + 1) * SSD_STATE]
        dt = _softplus(dt_ref[...] + dtb_ref[...])
        da_s[...] = jnp.exp(dt * (-jnp.exp(alog_ref[...])))
        dt_t = dt.T
        xs_t = xs.T
        for hh in range(SSD_HEADS):
            xdt_s[hh] = xs_t[hh * SSD_HEAD_DIM:(hh + 1) * SSD_HEAD_DIM] * dt_t[hh:hh + 1, :]
        sc = sc_ref[...]
        p = sc[:, SC_WIDTH:2 * SC_WIDTH] * sc[:, 2 * SC_WIDTH:]
        scw = scw_ref[...]
        yb_ref[...] = sc[:, :SC_WIDTH] * (
            scst_ref[0] * scw[0:1] + scst_ref[1] * scw[1:2] + p * scw[2:3])
        nscst_ref[0] = scst_ref[1]
        nscst_ref[1] = p

    lane_t = lax.broadcasted_iota(jnp.int32, (nb, LANES), 1)
    da_col = jnp.sum(jnp.where(lane_t == h, da_s[...], 0.0), axis=1, keepdims=True)
    dab_s[...] = jnp.broadcast_to(da_col, (nb, LANES))

    xt = xdt_s[h]
    for b in range(nb):
        xb_s[b] = jnp.broadcast_to(xt[:, b:b + 1], (SSD_HEAD_DIM, LANES))

    g = h // HEADS_PER_GROUP
    lane = lax.broadcasted_iota(jnp.int32, (SSD_HEAD_DIM, LANES), 1)
    yacc = jnp.zeros((SSD_HEAD_DIM, LANES), F32)
    for b in range(nb):
        hn = st_ref[b] * dab_s[b:b + 1, :] + xb_s[b] * bm_s[g, b:b + 1, :]
        nst_ref[b] = hn
        ycol = jnp.sum(hn * cm_s[g, b:b + 1, :], axis=1, keepdims=True)
        yacc = jnp.where(lane == b, ycol, yacc)
    yt_s[h] = yacc

    @pl.when(h == pl.num_programs(0) - 1)
    def _():
        xs = xs_s[...]
        y = yt_s[...].reshape(SSD_WIDTH, nb).T + xs * dexp_ref[...]
        y = y * _silu(z_ref[...])
        _group_norm_store(ya_ref, y, nw_ref[...])


def _dec_ssd(z, xbc, dt, sc, cst, scst, state, layer, prev_state, cw, cb, dtb, alog, dexp, nw, scw):
    nb = z.shape[0]
    consts = (cw, cb, dtb, alog, dexp, nw, scw)
    st_spec = pl.BlockSpec((None, nb, None, SSD_HEAD_DIM, SSD_STATE), lambda h: (layer, 0, h, 0, 0))
    carried = [] if prev_state is None else [prev_state]
    n_in = 7 + len(consts)
    return pl.pallas_call(
        _dec_ssd_body,
        grid=(SSD_HEADS,),
        in_specs=[_const_spec(a.shape) for a in (z, xbc, dt, sc, cst, scst)] + [st_spec]
        + [_const_spec(a.shape) for a in consts] + [pl.BlockSpec(memory_space=pl.ANY)] * len(carried),
        out_specs=[pl.BlockSpec((nb, SSD_WIDTH), lambda h: (0, 0)),
                   pl.BlockSpec((nb, SC_WIDTH), lambda h: (0, 0)),
                   pl.BlockSpec(cst.shape, lambda h: (0, 0, 0)),
                   pl.BlockSpec(scst.shape, lambda h: (0, 0, 0)),
                   st_spec],
        out_shape=[jax.ShapeDtypeStruct((nb, SSD_WIDTH), F32),
                   jax.ShapeDtypeStruct((nb, SC_WIDTH), F32),
                   jax.ShapeDtypeStruct(cst.shape, F32),
                   jax.ShapeDtypeStruct(scst.shape, F32),
                   jax.ShapeDtypeStruct(state.shape, F32)],
        scratch_shapes=[pltpu.VMEM((nb, SSD_WIDTH), F32),
                        pltpu.VMEM((SSD_HEADS, SSD_HEAD_DIM, nb), F32),
                        pltpu.VMEM((nb, LANES), F32),
                        pltpu.VMEM((nb, LANES), F32),
                        pltpu.VMEM((SSD_GROUPS, nb, SSD_STATE), F32),
                        pltpu.VMEM((SSD_GROUPS, nb, SSD_STATE), F32),
                        pltpu.VMEM((nb, SSD_HEAD_DIM, LANES), F32),
                        pltpu.VMEM((SSD_HEADS, SSD_HEAD_DIM, nb), F32)],
        input_output_aliases={n_in: 4} if carried else {},
        compiler_params=_params("arbitrary"),
        name="dec_ssd",
    )(z, xbc, dt, sc, cst, scst, state, *consts, *carried)


def _ffn_attn_body(pt_ref, x_ref, g_ref, wg_ref, wu_ref, wd_ref, qt_ref, bias_ref, u_ref, p_ref,
                   *refs):
    n_pages = (len(refs) - 3) // 2
    k_refs, v_refs = refs[:n_pages], refs[n_pages:2 * n_pages]
    o_ref, yo_ref, w_s = refs[2 * n_pages:]
    b = pl.program_id(0)
    nb = qt_ref.shape[1]

    @pl.when(b == 0)
    def _():
        yo_ref[...] = jnp.zeros_like(yo_ref)

    token = lax.broadcasted_iota(jnp.int32, (SB_WIDTH, nb), 1) == b
    q_col = jnp.sum(jnp.where(token, qt_ref[...], 0.0), axis=1, keepdims=True)
    sub = lax.broadcasted_iota(jnp.int32, (SB_HEADS, LANES), 0)

    def scores(p):
        zp = jnp.zeros((SB_HEADS, LANES), F32)
        for h in range(SB_HEADS):
            prod = k_refs[p][h] * q_col[h * SB_HEAD_DIM:(h + 1) * SB_HEAD_DIM]
            zp = jnp.where(sub == h, jnp.sum(prod, axis=0, keepdims=True), zp)
        return zp

    def weights(z_pages):
        z = jnp.concatenate(z_pages, axis=0) * SB_SCALE + bias_ref[...]
        sp = _softplus(z)
        lk = -sp
        n = z.shape[0]
        tot = jnp.broadcast_to(jnp.sum(lk, axis=1, keepdims=True), (n, LANES))
        tail = _dot_split(lk, u_ref[...], 3) + _dot_split_left(p_ref[...], tot, 3)
        w_s[...] = jnp.exp((z - sp) + tail)

    def values(h):
        acc = jnp.zeros((SB_HEAD_DIM, LANES), F32)
        for p in range(n_pages):
            r = p * SB_HEADS + h
            acc = acc + v_refs[p][h] * w_s[r:r + 1, :]
        return jnp.sum(acc, axis=1, keepdims=True)

    weights([scores(p) for p in range(n_pages)])
    o_col = jnp.concatenate([values(h) for h in range(SB_HEADS)], axis=0)
    yo_ref[...] = jnp.where(token, o_col, yo_ref[...])
    _ffn_body(x_ref, g_ref, wg_ref, wu_ref, wd_ref, o_ref)


def _ffn_attn(x, g, wg, wu, wd, page_table, q_t, cache_kt, cache_vt, layer, bias_col, u, pmat):
    n, d = x.shape
    nb, n_pages = page_table.shape
    page = cache_kt.shape[-1]
    tm = n // nb
    assert tm * nb == n and tm % SUBLANES == 0

    def page_spec(j):
        return pl.BlockSpec((None, None, SB_HEADS, SB_HEAD_DIM, page),
                            lambda b, pt: (layer, pt[b, j], 0, 0, 0))

    def const(a):
        return pl.BlockSpec(a.shape, lambda b, pt: (0,) * a.ndim, pipeline_mode=pl.Buffered(1))

    row = pl.BlockSpec((tm, d), lambda b, pt: (b, 0))
    grid_spec = pltpu.PrefetchScalarGridSpec(
        num_scalar_prefetch=1,
        grid=(nb,),
        in_specs=[row] + [const(a) for a in (g, wg, wu, wd, q_t, bias_col, u, pmat)]
        + [page_spec(j) for j in range(n_pages)] * 2,
        out_specs=[row, pl.BlockSpec((SB_WIDTH, nb), lambda b, pt: (0, 0))],
        scratch_shapes=[pltpu.VMEM((n_pages * SB_HEADS, LANES), F32)],
    )
    return pl.pallas_call(
        _ffn_attn_body,
        grid_spec=grid_spec,
        out_shape=[jax.ShapeDtypeStruct((n, d), F32), jax.ShapeDtypeStruct((SB_WIDTH, nb), F32)],
        compiler_params=_params("arbitrary"),
        name="ffn_attn",
    )(page_table, x, g, wg, wu, wd, q_t, bias_col, u, pmat,
      *([cache_kt] * n_pages), *([cache_vt] * n_pages))


def _tri01(n, strict_upper_rows):
    r = lax.broadcasted_iota(jnp.int32, (n, n), 0)
    c = lax.broadcasted_iota(jnp.int32, (n, n), 1)
    return ((r > c) if strict_upper_rows else (r >= c)).astype(BF16)


def kernel(x_prompt, x_sample, cache_sb_k, cache_sb_v, page_table, state_ssd, state_ssd_conv, state_sc_conv, ffn1_norm, ffn1_w_gate, ffn1_w_up, ffn1_w_down, mix_norm, w_in, gate_bias, ssd_conv_w, ssd_conv_b, ssd_dt_bias, ssd_a_log, ssd_d, ssd_norm, sc_conv_w, sb_q_norm, sb_k_norm, sb_logit_bias, w_proj_a, w_proj_b, w_proj_c, w_out, ffn2_norm, ffn2_w_gate, ffn2_w_up, ffn2_w_down):
    depth = w_in.shape[0]
    bsz, seq, d = x_prompt.shape
    dec_b = x_sample.shape[0]
    page = cache_sb_k.shape[2]
    n_pages = page_table.shape[1]
    cache_kt = cache_sb_k.transpose(0, 1, 3, 4, 2)
    cache_vt = cache_sb_v.transpose(0, 1, 3, 4, 2)

    tri_incl = _tri01(SSD_CHUNK, False)
    u_tq = _tri01(SB_TQ, True)
    u_page = _tri01(page, True)
    lane = jnp.arange(LANES)
    bd = ((lane[:, None] // SB_HEAD_DIM) == (lane[None, :] // SB_HEAD_DIM)).astype(BF16)
    col = jnp.arange(SSD_WIDTH)
    e_mat = (lane[:, None] == (col[None, :] // SSD_HEAD_DIM)).astype(BF16)
    r = jnp.arange(n_pages * SB_HEADS)
    pmat = ((r[:, None] % SB_HEADS == r[None, :] % SB_HEADS)
            & (r[None, :] // SB_HEADS > r[:, None] // SB_HEADS)).astype(BF16)

    def pad_lanes(a):
        return jnp.pad(a, ((0, 0), (0, LANES - a.shape[1])))

    xp = x_prompt.reshape(bsz * seq, d)
    xs = x_sample.reshape(dec_b * x_sample.shape[1], d)
    kv_p = kv_s = nst = None
    hst, cst, scst, ncst, nscst = [], [], [], [], []
    for l in range(depth):
        wl = w_in[l]
        c_dt = N_GATES + SSD_WIDTH + SSD_CONV_DIM
        c_q = c_dt + SSD_HEADS + 3 * SC_WIDTH
        w_packed = jnp.concatenate(
            [wl[:, :c_dt], wl[:, c_dt + SSD_HEADS:c_q + SB_WIDTH],
             pad_lanes(wl[:, c_dt:c_dt + SSD_HEADS])], axis=1).astype(BF16)
        wkv_t = wl[:, c_q + SB_WIDTH:].T.astype(BF16)
        f1 = (ffn1_norm[l][None], ffn1_w_gate[l].astype(BF16), ffn1_w_up[l].astype(BF16),
              ffn1_w_down[l].astype(BF16))
        f2 = (ffn2_norm[l][None], ffn2_w_gate[l].astype(BF16), ffn2_w_up[l].astype(BF16),
              ffn2_w_down[l].astype(BF16))
        ip = (mix_norm[l][None], w_packed, wkv_t, gate_bias[l][None],
              jnp.tile(sb_q_norm[l], SB_HEADS)[None], sb_k_norm[l][:, None], bd)
        ssd_p = (ssd_conv_w[l], ssd_conv_b[l][None], pad_lanes(ssd_dt_bias[l][None]),
                 pad_lanes(ssd_a_log[l][None]), jnp.repeat(ssd_d[l], SSD_HEAD_DIM)[None],
                 ssd_norm[l][None], sc_conv_w[l])
        mg = (w_proj_a[l].astype(BF16), w_proj_b[l].astype(BF16), w_proj_c[l].astype(BF16),
              w_out[l].astype(BF16))

        s1 = _ffn(xs, *f1)
        s_gates, s_z, s_xbc, s_dt, s_sc, s_q, *kv_s = _inproj(s1, dec_b, depth, l, kv_s, *ip)

        x1, yc_t = _ffn_attn(xp, *f1, page_table, s_q.T, cache_kt, cache_vt, l,
                             jnp.tile(sb_logit_bias[l], n_pages)[:, None], u_page, pmat)
        gates, z, xbc, dt, sc, q, *kv_p = _inproj(x1, seq, depth, l, kv_p, *ip)
        as_seq = lambda a: a.reshape(bsz, seq, a.shape[-1])
        ya, yb, h_l, cst8, scst8 = _seqmix(as_seq(z), as_seq(xbc), as_seq(dt), as_seq(sc),
                                            *ssd_p, e_mat, tri_incl)
        yc = _sb_attn(as_seq(q), *kv_p, l, sb_logit_bias[l], u_tq)
        xp = _merge_ffn(x1, gates, ya.reshape(-1, SSD_WIDTH), yb.reshape(-1, SC_WIDTH),
                        yc.reshape(-1, SB_WIDTH), *mg, *f2)
        hst.append(h_l)
        cst.append(cst8[:, SUBLANES - (SSD_CONV - 1):])
        scst.append(scst8[:, SUBLANES - (SC_CONV - 1):])

        ya, yb, ncst_l, nscst_l, nst = _dec_ssd(
            s_z, s_xbc, s_dt, s_sc, state_ssd_conv[l].transpose(1, 0, 2),
            state_sc_conv[l].transpose(1, 0, 2), state_ssd, l, nst, *ssd_p)
        xs = _merge_ffn(s1, s_gates, ya, yb, yc_t.T, *mg, *f2)
        ncst.append(ncst_l.transpose(1, 0, 2))
        nscst.append(nscst_l.transpose(1, 0, 2))

    kp, vp = (a.transpose(0, 1, 4, 2, 3) for a in kv_p)
    ks, vs = (a.transpose(0, 4, 1, 2, 3) for a in kv_s)
    return (xp.reshape(bsz, seq, d), xs.reshape(dec_b, x_sample.shape[1], d),
            kp, vp, ks, vs, jnp.stack(hst), nst, jnp.stack(cst), jnp.stack(ncst),
            jnp.stack(scst), jnp.stack(nscst))
```

```python
import functools

import jax
import jax.numpy as jnp
from jax import lax
from jax.experimental import pallas as pl
from jax.experimental.pallas import tpu as pltpu

F32 = jnp.float32
BF16 = jnp.bfloat16
EPS = 1e-6

LANES = 128
SUBLANES = 8
VMEM_LIMIT = 56 * 1024 * 1024

D_MODEL = 1024
SSD_HEADS = 16
SSD_HEAD_DIM = 64
SSD_WIDTH = SSD_HEADS * SSD_HEAD_DIM
SSD_GROUPS = 2
SSD_STATE = 128
SSD_CONV = 4
SSD_CONV_DIM = SSD_WIDTH + 2 * SSD_GROUPS * SSD_STATE
SSD_GROUP_WIDTH = SSD_WIDTH // SSD_GROUPS
HEADS_PER_GROUP = SSD_HEADS // SSD_GROUPS
SC_WIDTH = 512
SC_CONV = 3
SB_HEADS = 8
SB_HEAD_DIM = 64
SB_WIDTH = SB_HEADS * SB_HEAD_DIM
SB_SCALE = SB_HEAD_DIM ** -0.5
LOG2E = 1.4426950408889634
N_GATES = 3 * D_MODEL

C_GATE = 0
C_Z = C_GATE + N_GATES
C_XBC = C_Z + SSD_WIDTH
C_SC = C_XBC + SSD_CONV_DIM
C_DT = C_SC + 3 * SC_WIDTH
C_END = C_DT + LANES

FFN_TM = 512
FFN_CHUNKS = 2
INPROJ_TM = 256
MERGE_TM = 256
SSD_CHUNK = 128
SB_TQ = 256


def _dot(a, b):
    return jnp.dot(a, b, preferred_element_type=F32)


def _dot_nt(a, b):
    return lax.dot_general(a, b, (((1,), (1,)), ((), ())), preferred_element_type=F32)


def _bf16_pieces(a, n):
    pieces, r = [], a
    for i in range(n):
        p = r.astype(BF16)
        pieces.append(p)
        if i + 1 < n:
            r = r - p.astype(F32)
    return pieces


def _dot_split(a, b01, n):
    out = None
    for p in _bf16_pieces(a, n):
        d = _dot(p, b01)
        out = d if out is None else out + d
    return out


def _dot_split_left(a01, b, n):
    out = None
    for p in _bf16_pieces(b, n):
        d = _dot(a01, p)
        out = d if out is None else out + d
    return out


def _rmsnorm(x, g):
    return x * lax.rsqrt(jnp.mean(x * x, axis=-1, keepdims=True) + EPS) * g


def _silu(x):
    return x * jax.nn.sigmoid(x)


def _softplus(x):
    return jnp.maximum(x, 0.0) + jnp.log1p(jnp.exp(-jnp.abs(x)))


def _const_spec(shape):
    nd = len(shape)
    return pl.BlockSpec(shape, lambda *_: (0,) * nd, pipeline_mode=pl.Buffered(1))


def _layer_spec(a, layer):
    nd = a.ndim - 1
    return pl.BlockSpec((None,) + a.shape[1:], lambda *_: (layer,) + (0,) * nd,
                        pipeline_mode=pl.Buffered(1))


def _params(*sem):
    return pltpu.CompilerParams(dimension_semantics=sem, vmem_limit_bytes=VMEM_LIMIT)


def _ffn_value(x, g_ref, wg_ref, wu_ref, wd_ref):
    h = _rmsnorm(x, g_ref[...]).astype(BF16)
    c = wg_ref.shape[1] // FFN_CHUNKS
    y = None
    for i in range(FFN_CHUNKS):
        a = _dot(h, wg_ref[:, i * c:(i + 1) * c])
        u = _dot(h, wu_ref[:, i * c:(i + 1) * c])
        act = (_silu(a) * u).astype(BF16)
        yi = _dot(act, wd_ref[i * c:(i + 1) * c, :])
        y = yi if y is None else y + yi
    return x + 0.5 * y


def _ffn_body(x_ref, g_ref, wg_ref, wu_ref, wd_ref, o_ref):
    o_ref[...] = _ffn_value(x_ref[...], g_ref, wg_ref, wu_ref, wd_ref)


def _ffn(x, layer, g, wg, wu, wd):
    n, d = x.shape
    tm = min(FFN_TM, n)
    row = pl.BlockSpec((tm, d), lambda i: (i, 0))
    return pl.pallas_call(
        _ffn_body,
        grid=(n // tm,),
        in_specs=[row] + [_layer_spec(a, layer) for a in (g, wg, wu, wd)],
        out_specs=row,
        out_shape=jax.ShapeDtypeStruct((n, d), F32),
        compiler_params=_params("parallel"),
        name="ffn",
    )(x, g, wg, wu, wd)


def _inproj_body(x_ref, g_ref, w_ref, wqkv_ref, gb_ref, qn_ref, kn_ref, *rest):
    gates_ref, z_ref, xbc_ref, dt_ref, sc_ref, qt_ref, kt_ref, vt_ref = rest[-8:]
    h = _rmsnorm(x_ref[...], g_ref[...]).astype(BF16)

    def seg(lo, hi):
        return _dot(h, w_ref[:, lo:hi])

    gates_ref[...] = jax.nn.sigmoid(seg(C_GATE, C_Z) + gb_ref[...])
    z_ref[...] = seg(C_Z, C_XBC)
    xbc_ref[...] = seg(C_XBC, C_SC)
    sc_ref[...] = seg(C_SC, C_DT)
    dt_ref[...] = seg(C_DT, C_END)

    qkv_t = _dot_nt(wqkv_ref[...], h)
    tm = qkv_t.shape[1]

    def heads(s):
        return qkv_t[s * SB_WIDTH:(s + 1) * SB_WIDTH].reshape(SB_HEADS, SB_HEAD_DIM, tm)

    def head_norm(a, gain_ref):
        return a * lax.rsqrt(jnp.mean(a * a, axis=1, keepdims=True) + EPS) * gain_ref[...]

    qt_ref[...] = head_norm(heads(0), qn_ref)
    kt_ref[...] = head_norm(heads(1), kn_ref)
    vt_ref[...] = heads(2)


def _inproj(x, seq, depth, layer, prev_kv, g, w, wqkv_t, gate_bias, qn, kn):
    n, d = x.shape
    tm = min(INPROJ_TM, seq)
    nt = seq // tm

    def row(c):
        return pl.BlockSpec((tm, c), lambda i: (i, 0))

    kv_spec = pl.BlockSpec((None, None, SB_HEADS, SB_HEAD_DIM, tm),
                           lambda i: (layer, i // nt, 0, 0, i % nt))
    kv_shape = jax.ShapeDtypeStruct((depth, n // seq, SB_HEADS, SB_HEAD_DIM, seq), F32)
    q_spec = pl.BlockSpec((None, SB_HEADS, SB_HEAD_DIM, tm), lambda i: (i // nt, 0, 0, i % nt))
    q_shape = jax.ShapeDtypeStruct((n // seq, SB_HEADS, SB_HEAD_DIM, seq), F32)
    widths = (N_GATES, SSD_WIDTH, SSD_CONV_DIM, LANES, 3 * SC_WIDTH)
    consts = (g, w, wqkv_t, gate_bias, qn, kn)
    carried = [] if prev_kv is None else list(prev_kv)
    n_in = 1 + len(consts)
    return pl.pallas_call(
        _inproj_body,
        grid=(n // tm,),
        in_specs=[row(d)] + [_layer_spec(a, layer) for a in consts]
        + [pl.BlockSpec(memory_space=pl.ANY)] * len(carried),
        out_specs=[row(c) for c in widths] + [q_spec, kv_spec, kv_spec],
        out_shape=[jax.ShapeDtypeStruct((n, c), F32) for c in widths] + [q_shape, kv_shape, kv_shape],
        input_output_aliases={n_in: len(widths) + 1, n_in + 1: len(widths) + 2} if carried else {},
        compiler_params=_params("parallel"),
        name="inproj",
    )(x, *consts, *carried)


def _shift_rows(x, halo, k):
    r = pltpu.roll(x, k, axis=0)
    hk = pltpu.roll(halo, k, axis=0)
    rid = lax.broadcasted_iota(jnp.int32, hk.shape, 0)
    head = jnp.where(rid < k, hk, r[:SUBLANES])
    return jnp.concatenate([head, r[SUBLANES:]], axis=0)


def _causal_conv(x, halo, w):
    taps = w.shape[0]
    acc = _shift_rows(x, halo, taps - 1) * w[0:1]
    for k in range(1, taps - 1):
        acc = acc + _shift_rows(x, halo, taps - 1 - k) * w[k:k + 1]
    return acc + x * w[taps - 1:taps]


def _group_norm_store(dst_ref, y, nw):
    for g in range(SSD_GROUPS):
        sl = slice(g * SSD_GROUP_WIDTH, (g + 1) * SSD_GROUP_WIDTH)
        dst_ref[:, sl] = _rmsnorm(y[:, sl], nw[:, sl])


def _seqmix_body(z_ref, xbc_ref, xh_ref, dt_ref, sc_ref, sh_ref,
                 cw_ref, cb_ref, dtb_ref, alog_ref, dexp_ref, nw_ref, scw_ref, e_ref, tri_ref,
                 ya_ref, yb_ref, st_ref, cst_ref, scst_ref, hs_ref):
    t = pl.program_id(1)
    first = t == 0
    last = t == pl.num_programs(1) - 1
    L = xbc_ref.shape[0]

    @pl.when(first)
    def _():
        hs_ref[...] = jnp.zeros_like(hs_ref)

    xbc = xbc_ref[...]
    halo = jnp.where(first, 0.0, xh_ref[...])
    act = _silu(_causal_conv(xbc, halo, cw_ref[...]) + cb_ref[...])
    xs = act[:, :SSD_WIDTH]

    dt = _softplus(dt_ref[...] + dtb_ref[...])
    la = dt * (-jnp.exp(alog_ref[...]))
    cs = _dot_split_left(tri_ref[...], la, 3)
    cs_t = cs.T
    tot = cs[L - 1:L, :]
    to_end = jnp.exp(tot - cs)
    stack = jnp.concatenate(
        [dt, jnp.exp(cs), to_end, jnp.broadcast_to(jnp.exp(tot), (SUBLANES, LANES))], axis=0)
    ex = _dot_split(stack, e_ref[...], 3)
    dt_e, ecs_e, te_e, cd_e = ex[:L], ex[L:2 * L], ex[2 * L:3 * L], ex[3 * L:3 * L + 1]
    xd = xs * dt_e
    xd_b = xd.astype(BF16)
    xdte_b = (xd * te_e).astype(BF16)

    rid = lax.broadcasted_iota(jnp.int32, (L, L), 0)
    cid = lax.broadcasted_iota(jnp.int32, (L, L), 1)
    causal = rid >= cid
    low_half = lax.broadcasted_iota(jnp.int32, (L, LANES), 1) < SSD_HEAD_DIM

    y_groups = []
    for g in range(SSD_GROUPS):
        gs = slice(g * SSD_GROUP_WIDTH, (g + 1) * SSD_GROUP_WIDTH)
        b_g = act[:, SSD_WIDTH + g * SSD_STATE:SSD_WIDTH + (g + 1) * SSD_STATE]
        c0 = SSD_WIDTH + SSD_GROUPS * SSD_STATE
        c_g = act[:, c0 + g * SSD_STATE:c0 + (g + 1) * SSD_STATE]
        b_gb, c_gb = b_g.astype(BF16), c_g.astype(BF16)
        cb = _dot_nt(c_gb, b_gb)
        h_in = hs_ref[g]
        y_off = _dot(c_gb, h_in.astype(BF16)) * ecs_e[:, gs]
        pieces = []
        for pr in range(HEADS_PER_GROUP // 2):
            ha = g * HEADS_PER_GROUP + 2 * pr
            xp = xd_b[:, ha * SSD_HEAD_DIM:(ha + 2) * SSD_HEAD_DIM]
            yd = []
            for hh in (ha, ha + 1):
                seg = cs[:, hh:hh + 1] - cs_t[hh:hh + 1, :]
                m = cb * jnp.exp(jnp.where(causal, seg, -jnp.inf))
                yd.append(_dot(m.astype(BF16), xp))
            pieces.append(jnp.where(low_half, yd[0], yd[1]))
        y_groups.append(jnp.concatenate(pieces, axis=1) + y_off)
        s_new = _dot(b_g.T.astype(BF16), xdte_b[:, gs])
        hs_ref[g] = h_in * cd_e[:, gs] + s_new

    y = jnp.concatenate(y_groups, axis=1) + xs * dexp_ref[...]
    y = y * _silu(z_ref[...])
    _group_norm_store(ya_ref, y, nw_ref[...])

    sc = sc_ref[...]
    p = sc[:, SC_WIDTH:2 * SC_WIDTH] * sc[:, 2 * SC_WIDTH:]
    sh = sh_ref[...]
    ph = jnp.where(first, 0.0, sh[:, SC_WIDTH:2 * SC_WIDTH] * sh[:, 2 * SC_WIDTH:])
    yb_ref[...] = sc[:, :SC_WIDTH] * _causal_conv(p, ph, scw_ref[...])

    @pl.when(last)
    def _():
        for g in range(SSD_GROUPS):
            st_ref[g * HEADS_PER_GROUP:(g + 1) * HEADS_PER_GROUP] = hs_ref[g].T.reshape(
                HEADS_PER_GROUP, SSD_HEAD_DIM, SSD_STATE)
        cst_ref[...] = xbc[L - SUBLANES:]
        scst_ref[...] = p[L - SUBLANES:]


def _seqmix(z, xbc, dt, sc, layer, cw, cb, dtb, alog, dexp, nw, scw, e_mat, tri):
    b, t, _ = z.shape
    L = SSD_CHUNK
    hb = L // SUBLANES

    def cur(c):
        return pl.BlockSpec((None, L, c), lambda i, j: (i, j, 0))

    def halo(c):
        return pl.BlockSpec((None, SUBLANES, c), lambda i, j: (i, jnp.maximum(j * hb - 1, 0), 0))

    def per_seq(shape):
        return pl.BlockSpec((None,) + shape, lambda i, j: (i,) + (0,) * len(shape))

    consts = (cw, cb, dtb, alog, dexp, nw, scw, e_mat, tri)
    return pl.pallas_call(
        _seqmix_body,
        grid=(b, t // L),
        in_specs=[cur(SSD_WIDTH), cur(SSD_CONV_DIM), halo(SSD_CONV_DIM), cur(LANES),
                  cur(3 * SC_WIDTH), halo(3 * SC_WIDTH)]
        + [_layer_spec(a, layer) for a in consts[:-2]] + [_const_spec(a.shape) for a in consts[-2:]],
        out_specs=[cur(SSD_WIDTH), cur(SC_WIDTH), per_seq((SSD_HEADS, SSD_HEAD_DIM, SSD_STATE)),
                   per_seq((SUBLANES, SSD_CONV_DIM)), per_seq((SUBLANES, SC_WIDTH))],
        out_shape=[jax.ShapeDtypeStruct((b, t, SSD_WIDTH), F32),
                   jax.ShapeDtypeStruct((b, t, SC_WIDTH), F32),
                   jax.ShapeDtypeStruct((b, SSD_HEADS, SSD_HEAD_DIM, SSD_STATE), F32),
                   jax.ShapeDtypeStruct((b, SUBLANES, SSD_CONV_DIM), F32),
                   jax.ShapeDtypeStruct((b, SUBLANES, SC_WIDTH), F32)],
        scratch_shapes=[pltpu.VMEM((SSD_GROUPS, SSD_STATE, SSD_GROUP_WIDTH), F32)],
        compiler_params=_params("parallel", "arbitrary"),
        name="seqmix",
    )(z, xbc, xbc, dt, sc, sc, *consts)


def _sb_body(bias_ref, q_ref, k_ref, v_ref, u_ref, o_ref, *, layer):
    hp = pl.program_id(1)
    i = pl.program_id(2)
    tq = q_ref.shape[-1]
    q2 = q_ref[...].reshape(LANES, tq).T
    low_half = lax.broadcasted_iota(jnp.int32, (tq, LANES), 1) < SB_HEAD_DIM
    q_heads = (jnp.where(low_half, q2, 0.0).astype(BF16), jnp.where(low_half, 0.0, q2).astype(BF16))
    biases = (bias_ref[layer, 2 * hp] * LOG2E, bias_ref[layer, 2 * hp + 1] * LOG2E)
    u = u_ref[...]

    def blocks(js, carry, masks):
        acc, cs = carry[0], list(carry[1:])
        kv = []
        for j in js:
            off = pl.multiple_of(j * tq, tq)
            kv.append((k_ref[:, :, pl.ds(off, tq)].reshape(LANES, tq).astype(BF16),
                       v_ref[:, :, pl.ds(off, tq)].reshape(LANES, tq).astype(BF16)))
        chains = [(n, h) for n in range(len(js)) for h in range(2)]
        z2 = [_dot(q_heads[h], kv[n][0]) * (SB_SCALE * LOG2E) + biases[h] for n, h in chains]
        sp2, lb2 = [], []
        for (n, h), z in zip(chains, z2):
            sp = jnp.maximum(z, 0.0) + jnp.log2(1.0 + jnp.exp2(-jnp.abs(z)))
            lb2.append(z - sp)
            sp2.append(sp if masks[n] is None else jnp.where(masks[n], sp, 0.0))
        tails = [_dot(sp.astype(BF16), u) for sp in sp2]
        ws = []
        for (n, h), lb, tail, sp in zip(chains, lb2, tails, sp2):
            w = jnp.exp2(lb - (tail + cs[h]))
            ws.append(w if masks[n] is None else jnp.where(masks[n], w, 0.0))
            cs[h] = cs[h] + jnp.sum(sp, axis=1, keepdims=True)
        pv = [_dot_nt(w.astype(BF16), kv[n][1]) for (n, h), w in zip(chains, ws)]
        for n in range(len(js)):
            acc = acc + jnp.where(low_half, pv[2 * n], pv[2 * n + 1])
        return (acc, *cs)

    rid = lax.broadcasted_iota(jnp.int32, (tq, tq), 0)
    cid = lax.broadcasted_iota(jnp.int32, (tq, tq), 1)
    diag = cid < rid
    init = (jnp.zeros((tq, LANES), F32), jnp.zeros((tq, 1), F32), jnp.zeros((tq, 1), F32))
    carry = lax.cond(i == 0, lambda cr: blocks([i], cr, [diag]),
                     lambda cr: blocks([i, i - 1], cr, [diag, None]), init)
    rest = jnp.maximum(i - 1, 0)
    odd = rest % 2
    carry = lax.cond(odd == 1, lambda cr: blocks([i - 2], cr, [None]), lambda cr: cr, carry)
    top = i - 2 - odd
    acc, _, _ = lax.fori_loop(
        0, rest // 2,
        lambda p, cr: blocks([top - 2 * p, top - 2 * p - 1], cr, [None, None]), carry)
    o_ref[...] = acc


def _sb_attn(q_t, k_t, v_t, layer, bias, u):
    b, _, _, t = q_t.shape
    w = SB_WIDTH
    tq = SB_TQ
    qspec = pl.BlockSpec((None, 2, SB_HEAD_DIM, tq), lambda bi, hp, i: (bi, hp, 0, i))
    ospec = pl.BlockSpec((None, tq, LANES), lambda bi, hp, i: (bi, i, hp))
    kvspec = pl.BlockSpec((None, None, 2, SB_HEAD_DIM, t), lambda bi, hp, i: (layer, bi, hp, 0, 0))
    return pl.pallas_call(
        functools.partial(_sb_body, layer=layer),
        grid=(b, w // LANES, t // tq),
        in_specs=[pl.BlockSpec(memory_space=pltpu.SMEM), qspec, kvspec, kvspec,
                  _const_spec(u.shape)],
        out_specs=ospec,
        out_shape=jax.ShapeDtypeStruct((b, t, w), F32),
        compiler_params=_params("parallel", "parallel", "arbitrary"),
        name="sb_attn",
    )(bias, q_t, k_t, v_t, u)


def _merge_ffn_body(x_ref, gates_ref, ya_ref, yb_ref, yc_ref, wa_ref, wb_ref, wc_ref, wo_ref,
                    g_ref, wg_ref, wu_ref, wd_ref, o_ref):
    gates = gates_ref[...]
    m = (gates[:, :D_MODEL] * _dot(ya_ref[...].astype(BF16), wa_ref[...])
         + gates[:, D_MODEL:2 * D_MODEL] * _dot(yb_ref[...].astype(BF16), wb_ref[...])
         + gates[:, 2 * D_MODEL:] * _dot(yc_ref[...].astype(BF16), wc_ref[...]))
    x2 = x_ref[...] + _dot(m.astype(BF16), wo_ref[...])
    o_ref[...] = _ffn_value(x2, g_ref, wg_ref, wu_ref, wd_ref)


def _merge_ffn(x, gates, ya, yb, yc, layer, wa, wb, wc, wo, g, wg, wu, wd):
    n, d = x.shape
    tm = min(MERGE_TM, n)

    def row(c):
        return pl.BlockSpec((tm, c), lambda i: (i, 0))

    consts = (wa, wb, wc, wo, g, wg, wu, wd)
    return pl.pallas_call(
        _merge_ffn_body,
        grid=(n // tm,),
        in_specs=[row(d), row(N_GATES), row(SSD_WIDTH), row(SC_WIDTH), row(SB_WIDTH)]
        + [_layer_spec(a, layer) for a in consts],
        out_specs=row(d),
        out_shape=jax.ShapeDtypeStruct((n, d), F32),
        compiler_params=_params("parallel"),
        name="merge_ffn",
    )(x, gates, ya, yb, yc, *consts)


def _dec_ssd_body(z_ref, xbc_ref, dt_ref, sc_ref, cst_ref, scst_ref, st_ref,
                  cw_ref, cb_ref, dtb_ref, alog_ref, dexp_ref, nw_ref, scw_ref, *rest):
    (ya_ref, yb_ref, ncst_ref, nscst_ref, nst_ref,
     xs_s, xdt_s, da_s, dab_s, bm_s, cm_s, xb_s, yt_s) = rest[-13:]
    h = pl.program_id(0)
    nb = z_ref.shape[0]

    @pl.when(h == 0)
    def _():
        xbc = xbc_ref[...]
        cw = cw_ref[...]
        conv = (cst_ref[0] * cw[0:1] + cst_ref[1] * cw[1:2] + cst_ref[2] * cw[2:3]
                + xbc * cw[3:4])
        ncst_ref[0] = cst_ref[1]
        ncst_ref[1] = cst_ref[2]
        ncst_ref[2] = xbc
        act = _silu(conv + cb_ref[...])
        xs = act[:, :SSD_WIDTH]
        xs_s[...] = xs
        c0 = SSD_WIDTH + SSD_GROUPS * SSD_STATE
        for g in range(SSD_GROUPS):
            bm_s[g] = act[:, SSD_WIDTH + g * SSD_STATE:SSD_WIDTH + (g + 1) * SSD_STATE]
            cm_s[g] = act[:, c0 + g * SSD_STATE:c0 + (g + 1) * SSD_STATE]
        dt = _softplus(dt_ref[...] + dtb_ref[...])
        da_s[...] = jnp.exp(dt * (-jnp.exp(alog_ref[...])))
        dt_t = dt.T
        xs_t = xs.T
        for hh in range(SSD_HEADS):
            xdt_s[hh] = xs_t[hh * SSD_HEAD_DIM:(hh + 1) * SSD_HEAD_DIM] * dt_t[hh:hh + 1, :]
        sc = sc_ref[...]
        p = sc[:, SC_WIDTH:2 * SC_WIDTH] * sc[:, 2 * SC_WIDTH:]
        scw = scw_ref[...]
        yb_ref[...] = sc[:, :SC_WIDTH] * (
            scst_ref[0] * scw[0:1] + scst_ref[1] * scw[1:2] + p * scw[2:3])
        nscst_ref[0] = scst_ref[1]
        nscst_ref[1] = p

    lane_t = lax.broadcasted_iota(jnp.int32, (nb, LANES), 1)
    da_col = jnp.sum(jnp.where(lane_t == h, da_s[...], 0.0), axis=1, keepdims=True)
    dab_s[...] = jnp.broadcast_to(da_col, (nb, LANES))

    xt = xdt_s[h]
    for b in range(nb):
        xb_s[b] = jnp.broadcast_to(xt[:, b:b + 1], (SSD_HEAD_DIM, LANES))

    g = h // HEADS_PER_GROUP
    lane = lax.broadcasted_iota(jnp.int32, (SSD_HEAD_DIM, LANES), 1)
    yacc = jnp.zeros((SSD_HEAD_DIM, LANES), F32)
    for b in range(nb):
        hn = st_ref[b] * dab_s[b:b + 1, :] + xb_s[b] * bm_s[g, b:b + 1, :]
        nst_ref[b] = hn
        ycol = jnp.sum(hn * cm_s[g, b:b + 1, :], axis=1, keepdims=True)
        yacc = jnp.where(lane == b, ycol, yacc)
    yt_s[h] = yacc

    @pl.when(h == pl.num_programs(0) - 1)
    def _():
        xs = xs_s[...]
        y = yt_s[...].reshape(SSD_WIDTH, nb).T + xs * dexp_ref[...]
        y = y * _silu(z_ref[...])
        _group_norm_store(ya_ref, y, nw_ref[...])


def _dec_ssd(z, xbc, dt, sc, cst, scst, state, layer, prev_state, cw, cb, dtb, alog, dexp, nw, scw):
    nb = z.shape[0]
    consts = (cw, cb, dtb, alog, dexp, nw, scw)
    st_spec = pl.BlockSpec((None, nb, None, SSD_HEAD_DIM, SSD_STATE), lambda h: (layer, 0, h, 0, 0))
    cst_shape, scst_shape = cst.shape[1:], scst.shape[1:]
    carried = [] if prev_state is None else [prev_state]
    n_in = 7 + len(consts)
    return pl.pallas_call(
        _dec_ssd_body,
        grid=(SSD_HEADS,),
        in_specs=[_const_spec(a.shape) for a in (z, xbc, dt, sc)]
        + [_layer_spec(a, layer) for a in (cst, scst)] + [st_spec]
        + [_layer_spec(a, layer) for a in consts] + [pl.BlockSpec(memory_space=pl.ANY)] * len(carried),
        out_specs=[pl.BlockSpec((nb, SSD_WIDTH), lambda h: (0, 0)),
                   pl.BlockSpec((nb, SC_WIDTH), lambda h: (0, 0)),
                   pl.BlockSpec(cst_shape, lambda h: (0, 0, 0)),
                   pl.BlockSpec(scst_shape, lambda h: (0, 0, 0)),
                   st_spec],
        out_shape=[jax.ShapeDtypeStruct((nb, SSD_WIDTH), F32),
                   jax.ShapeDtypeStruct((nb, SC_WIDTH), F32),
                   jax.ShapeDtypeStruct(cst_shape, F32),
                   jax.ShapeDtypeStruct(scst_shape, F32),
                   jax.ShapeDtypeStruct(state.shape, F32)],
        scratch_shapes=[pltpu.VMEM((nb, SSD_WIDTH), F32),
                        pltpu.VMEM((SSD_HEADS, SSD_HEAD_DIM, nb), F32),
                        pltpu.VMEM((nb, LANES), F32),
                        pltpu.VMEM((nb, LANES), F32),
                        pltpu.VMEM((SSD_GROUPS, nb, SSD_STATE), F32),
                        pltpu.VMEM((SSD_GROUPS, nb, SSD_STATE), F32),
                        pltpu.VMEM((nb, SSD_HEAD_DIM, LANES), F32),
                        pltpu.VMEM((SSD_HEADS, SSD_HEAD_DIM, nb), F32)],
        input_output_aliases={n_in: 4} if carried else {},
        compiler_params=_params("arbitrary"),
        name="dec_ssd",
    )(z, xbc, dt, sc, cst, scst, state, *consts, *carried)


def _ffn_attn_body(pt_ref, x_ref, g_ref, wg_ref, wu_ref, wd_ref, qt_ref, bias_ref, u_ref, p_ref,
                   *refs):
    n_pages = (len(refs) - 3) // 2
    k_refs, v_refs = refs[:n_pages], refs[n_pages:2 * n_pages]
    o_ref, yo_ref, w_s = refs[2 * n_pages:]
    b = pl.program_id(0)
    nb = qt_ref.shape[1]

    @pl.when(b == 0)
    def _():
        yo_ref[...] = jnp.zeros_like(yo_ref)

    token = lax.broadcasted_iota(jnp.int32, (SB_WIDTH, nb), 1) == b
    q_col = jnp.sum(jnp.where(token, qt_ref[...], 0.0), axis=1, keepdims=True)
    sub = lax.broadcasted_iota(jnp.int32, (SB_HEADS, LANES), 0)

    def scores(p):
        zp = jnp.zeros((SB_HEADS, LANES), F32)
        for h in range(SB_HEADS):
            prod = k_refs[p][h] * q_col[h * SB_HEAD_DIM:(h + 1) * SB_HEAD_DIM]
            zp = jnp.where(sub == h, jnp.sum(prod, axis=0, keepdims=True), zp)
        return zp

    def weights(z_pages):
        z = jnp.concatenate(z_pages, axis=0) * SB_SCALE + bias_ref[...]
        sp = _softplus(z)
        lk = -sp
        n = z.shape[0]
        tot = jnp.broadcast_to(jnp.sum(lk, axis=1, keepdims=True), (n, LANES))
        tail = _dot_split(lk, u_ref[...], 3) + _dot_split_left(p_ref[...], tot, 3)
        w_s[...] = jnp.exp((z - sp) + tail)

    def values(h):
        acc = jnp.zeros((SB_HEAD_DIM, LANES), F32)
        for p in range(n_pages):
            r = p * SB_HEADS + h
            acc = acc + v_refs[p][h] * w_s[r:r + 1, :]
        return jnp.sum(acc, axis=1, keepdims=True)

    weights([scores(p) for p in range(n_pages)])
    o_col = jnp.concatenate([values(h) for h in range(SB_HEADS)], axis=0)
    yo_ref[...] = jnp.where(token, o_col, yo_ref[...])
    _ffn_body(x_ref, g_ref, wg_ref, wu_ref, wd_ref, o_ref)


def _ffn_attn(x, layer, g, wg, wu, wd, page_table, q_t, cache_kt, cache_vt, bias_col, u, pmat):
    n, d = x.shape
    nb, n_pages = page_table.shape
    page = cache_kt.shape[-1]
    tm = n // nb
    assert tm * nb == n and tm % SUBLANES == 0

    def page_spec(j):
        return pl.BlockSpec((None, None, SB_HEADS, SB_HEAD_DIM, page),
                            lambda b, pt: (layer, pt[b, j], 0, 0, 0))

    def const(a):
        return pl.BlockSpec(a.shape, lambda b, pt: (0,) * a.ndim, pipeline_mode=pl.Buffered(1))

    row = pl.BlockSpec((tm, d), lambda b, pt: (b, 0))
    grid_spec = pltpu.PrefetchScalarGridSpec(
        num_scalar_prefetch=1,
        grid=(nb,),
        in_specs=[row] + [_layer_spec(a, layer) for a in (g, wg, wu, wd)] + [const(q_t)]
        + [_layer_spec(bias_col, layer), const(u), const(pmat)]
        + [page_spec(j) for j in range(n_pages)] * 2,
        out_specs=[row, pl.BlockSpec((SB_WIDTH, nb), lambda b, pt: (0, 0))],
        scratch_shapes=[pltpu.VMEM((n_pages * SB_HEADS, LANES), F32)],
    )
    return pl.pallas_call(
        _ffn_attn_body,
        grid_spec=grid_spec,
        out_shape=[jax.ShapeDtypeStruct((n, d), F32), jax.ShapeDtypeStruct((SB_WIDTH, nb), F32)],
        compiler_params=_params("arbitrary"),
        name="ffn_attn",
    )(page_table, x, g, wg, wu, wd, q_t, bias_col, u, pmat,
      *([cache_kt] * n_pages), *([cache_vt] * n_pages))


def _tri01(n, strict_upper_rows):
    r = lax.broadcasted_iota(jnp.int32, (n, n), 0)
    c = lax.broadcasted_iota(jnp.int32, (n, n), 1)
    return ((r > c) if strict_upper_rows else (r >= c)).astype(BF16)


def kernel(x_prompt, x_sample, cache_sb_k, cache_sb_v, page_table, state_ssd, state_ssd_conv, state_sc_conv, ffn1_norm, ffn1_w_gate, ffn1_w_up, ffn1_w_down, mix_norm, w_in, gate_bias, ssd_conv_w, ssd_conv_b, ssd_dt_bias, ssd_a_log, ssd_d, ssd_norm, sc_conv_w, sb_q_norm, sb_k_norm, sb_logit_bias, w_proj_a, w_proj_b, w_proj_c, w_out, ffn2_norm, ffn2_w_gate, ffn2_w_up, ffn2_w_down):
    depth = w_in.shape[0]
    bsz, seq, d = x_prompt.shape
    dec_b = x_sample.shape[0]
    page = cache_sb_k.shape[2]
    n_pages = page_table.shape[1]
    cache_kt = cache_sb_k.transpose(0, 1, 3, 4, 2)
    cache_vt = cache_sb_v.transpose(0, 1, 3, 4, 2)

    tri_incl = _tri01(SSD_CHUNK, False)
    u_tq = _tri01(SB_TQ, True)
    u_page = _tri01(page, True)
    lane = jnp.arange(LANES)
    col = jnp.arange(SSD_WIDTH)
    e_mat = (lane[:, None] == (col[None, :] // SSD_HEAD_DIM)).astype(BF16)
    r = jnp.arange(n_pages * SB_HEADS)
    pmat = ((r[:, None] % SB_HEADS == r[None, :] % SB_HEADS)
            & (r[None, :] // SB_HEADS > r[:, None] // SB_HEADS)).astype(BF16)

    def pad_lanes(a):
        return jnp.pad(a, ((0, 0), (0, LANES - a.shape[1])))

    row3 = lambda a: a[:, None, :]
    c_dt = N_GATES + SSD_WIDTH + SSD_CONV_DIM
    c_q = c_dt + SSD_HEADS + 3 * SC_WIDTH
    w_packed = jnp.concatenate(
        [w_in[:, :, :c_dt], w_in[:, :, c_dt + SSD_HEADS:c_q],
         jnp.pad(w_in[:, :, c_dt:c_dt + SSD_HEADS], ((0, 0), (0, 0), (0, LANES - SSD_HEADS)))],
        axis=2).astype(BF16)
    wqkv_t = w_in[:, :, c_q:].transpose(0, 2, 1).astype(BF16)
    f1 = (row3(ffn1_norm), ffn1_w_gate.astype(BF16), ffn1_w_up.astype(BF16), ffn1_w_down.astype(BF16))
    f2 = (row3(ffn2_norm), ffn2_w_gate.astype(BF16), ffn2_w_up.astype(BF16), ffn2_w_down.astype(BF16))
    ip = (row3(mix_norm), w_packed, wqkv_t, row3(gate_bias), sb_q_norm[:, :, None],
          sb_k_norm[:, :, None])
    ssd_p = (ssd_conv_w, row3(ssd_conv_b), row3(pad_lanes(ssd_dt_bias)), row3(pad_lanes(ssd_a_log)),
             row3(jnp.repeat(ssd_d, SSD_HEAD_DIM, axis=1)), row3(ssd_norm), sc_conv_w)
    mg = (w_proj_a.astype(BF16), w_proj_b.astype(BF16), w_proj_c.astype(BF16), w_out.astype(BF16))
    bias_col = jnp.tile(sb_logit_bias, (1, n_pages))[:, :, None]
    cst_in = state_ssd_conv.transpose(0, 2, 1, 3)
    scst_in = state_sc_conv.transpose(0, 2, 1, 3)

    xp = x_prompt.reshape(bsz * seq, d)
    xs = x_sample.reshape(dec_b * x_sample.shape[1], d)
    kv_p = kv_s = nst = None
    hst, cst, scst, ncst, nscst = [], [], [], [], []
    as_seq = lambda a: a.reshape(bsz, seq, a.shape[-1])
    for l in range(depth):
        s1 = _ffn(xs, l, *f1)
        s_gates, s_z, s_xbc, s_dt, s_sc, s_qt, *kv_s = _inproj(s1, dec_b, depth, l, kv_s, *ip)

        x1, yc_t = _ffn_attn(xp, l, *f1, page_table, s_qt.reshape(SB_WIDTH, dec_b), cache_kt, cache_vt,
                             bias_col, u_page, pmat)
        gates, z, xbc, dt, sc, q_t, *kv_p = _inproj(x1, seq, depth, l, kv_p, *ip)
        ya, yb, h_l, cst8, scst8 = _seqmix(as_seq(z), as_seq(xbc), as_seq(dt), as_seq(sc), l,
                                            *ssd_p, e_mat, tri_incl)
        yc = _sb_attn(q_t, *kv_p, l, sb_logit_bias, u_tq)
        xp = _merge_ffn(x1, gates, ya.reshape(-1, SSD_WIDTH), yb.reshape(-1, SC_WIDTH),
                        yc.reshape(-1, SB_WIDTH), l, *mg, *f2)
        hst.append(h_l)
        cst.append(cst8[:, SUBLANES - (SSD_CONV - 1):])
        scst.append(scst8[:, SUBLANES - (SC_CONV - 1):])

        ya, yb, ncst_l, nscst_l, nst = _dec_ssd(s_z, s_xbc, s_dt, s_sc, cst_in, scst_in,
                                                state_ssd, l, nst, *ssd_p)
        xs = _merge_ffn(s1, s_gates, ya, yb, yc_t.T, l, *mg, *f2)
        ncst.append(ncst_l)
        nscst.append(nscst_l)

    kp, vp = (a.transpose(0, 1, 4, 2, 3) for a in kv_p)
    ks, vs = (a.transpose(0, 4, 1, 2, 3) for a in kv_s)
    return (xp.reshape(bsz, seq, d), xs.reshape(dec_b, x_sample.shape[1], d),
            kp, vp, ks, vs, jnp.stack(hst), nst, jnp.stack(cst),
            jnp.stack(ncst).transpose(0, 2, 1, 3), jnp.stack(scst),
            jnp.stack(nscst).transpose(0, 2, 1, 3))
```

```python
import functools

import jax
import jax.numpy as jnp
from jax import lax
from jax.experimental import pallas as pl
from jax.experimental.pallas import tpu as pltpu

F32 = jnp.float32
BF16 = jnp.bfloat16
EPS = 1e-6

LANES = 128
SUBLANES = 8
VMEM_LIMIT = 56 * 1024 * 1024

D_MODEL = 1024
SSD_HEADS = 16
SSD_HEAD_DIM = 64
SSD_WIDTH = SSD_HEADS * SSD_HEAD_DIM
SSD_GROUPS = 2
SSD_STATE = 128
SSD_CONV = 4
SSD_CONV_DIM = SSD_WIDTH + 2 * SSD_GROUPS * SSD_STATE
SSD_GROUP_WIDTH = SSD_WIDTH // SSD_GROUPS
HEADS_PER_GROUP = SSD_HEADS // SSD_GROUPS
SC_WIDTH = 512
SC_CONV = 3
SB_HEADS = 8
SB_HEAD_DIM = 64
SB_WIDTH = SB_HEADS * SB_HEAD_DIM
SB_SCALE = SB_HEAD_DIM ** -0.5
LOG2E = 1.4426950408889634
N_GATES = 3 * D_MODEL

R_GATE = 0
R_Z = R_GATE + N_GATES
R_XBC = R_Z + SSD_WIDTH
R_DT = R_XBC + SSD_CONV_DIM
R_SC = R_DT + SSD_HEADS
R_QKV = R_SC + 3 * SC_WIDTH
R_END = R_QKV + 3 * SB_WIDTH

FFN_TM = 512
FFN_CHUNKS = 2
INPROJ_TM = 256
MERGE_TM = 256
SSD_CHUNK = 128
SB_TQ = 256


def _dot(a, b):
    return jnp.dot(a, b, preferred_element_type=F32)


def _dot_nt(a, b):
    return lax.dot_general(a, b, (((1,), (1,)), ((), ())), preferred_element_type=F32)


def _bf16_pieces(a, n):
    pieces, r = [], a
    for i in range(n):
        p = r.astype(BF16)
        pieces.append(p)
        if i + 1 < n:
            r = r - p.astype(F32)
    return pieces


def _dot_split(a, b01, n):
    out = None
    for p in _bf16_pieces(a, n):
        d = _dot(p, b01)
        out = d if out is None else out + d
    return out


def _dot_split_left(a01, b, n):
    out = None
    for p in _bf16_pieces(b, n):
        d = _dot(a01, p)
        out = d if out is None else out + d
    return out


def _rmsnorm(x, g):
    return x * lax.rsqrt(jnp.mean(x * x, axis=-1, keepdims=True) + EPS) * g


def _silu(x):
    return x * jax.nn.sigmoid(x)


def _softplus(x):
    return jnp.maximum(x, 0.0) + jnp.log1p(jnp.exp(-jnp.abs(x)))


def _const_spec(shape):
    nd = len(shape)
    return pl.BlockSpec(shape, lambda *_: (0,) * nd, pipeline_mode=pl.Buffered(1))


def _layer_spec(a, layer):
    nd = a.ndim - 1
    return pl.BlockSpec((None,) + a.shape[1:], lambda *_: (layer,) + (0,) * nd,
                        pipeline_mode=pl.Buffered(1))


def _params(*sem):
    return pltpu.CompilerParams(dimension_semantics=sem, vmem_limit_bytes=VMEM_LIMIT)


def _ffn_value(x, g_ref, wg_ref, wu_ref, wd_ref):
    h = _rmsnorm(x, g_ref[...]).astype(BF16)
    c = wg_ref.shape[1] // FFN_CHUNKS
    y = None
    for i in range(FFN_CHUNKS):
        a = _dot(h, wg_ref[:, i * c:(i + 1) * c])
        u = _dot(h, wu_ref[:, i * c:(i + 1) * c])
        act = (_silu(a) * u).astype(BF16)
        yi = _dot(act, wd_ref[i * c:(i + 1) * c, :])
        y = yi if y is None else y + yi
    return x + 0.5 * y


def _ffn_body(x_ref, g_ref, wg_ref, wu_ref, wd_ref, o_ref):
    o_ref[...] = _ffn_value(x_ref[...], g_ref, wg_ref, wu_ref, wd_ref)


def _ffn(x, layer, g, wg, wu, wd):
    n, d = x.shape
    tm = min(FFN_TM, n)
    row = pl.BlockSpec((tm, d), lambda i: (i, 0))
    return pl.pallas_call(
        _ffn_body,
        grid=(n // tm,),
        in_specs=[row] + [_layer_spec(a, layer) for a in (g, wg, wu, wd)],
        out_specs=row,
        out_shape=jax.ShapeDtypeStruct((n, d), F32),
        compiler_params=_params("parallel"),
        name="ffn",
    )(x, g, wg, wu, wd)


def _inproj_body(x_ref, g_ref, w_ref, gb_ref, qn_ref, kn_ref, *rest):
    gates_ref, z_ref, xbc_ref, dt_ref, sc_ref, qt_ref, kt_ref, vt_ref = rest[-8:]
    h = _rmsnorm(x_ref[...], g_ref[...]).astype(BF16)

    def seg(lo, hi):
        return _dot_nt(h, w_ref[lo:hi, :])

    gates_ref[...] = jax.nn.sigmoid(seg(R_GATE, R_Z) + gb_ref[...])
    z_ref[...] = seg(R_Z, R_XBC)
    xbc_ref[...] = seg(R_XBC, R_DT)
    sc_ref[...] = seg(R_SC, R_QKV)
    dt_ref[...] = jnp.zeros_like(dt_ref)
    dt_ref[:, :SSD_HEADS] = seg(R_DT, R_SC)

    qkv_t = _dot_nt(w_ref[R_QKV:R_END, :], h)
    tm = qkv_t.shape[1]

    def heads(s):
        return qkv_t[s * SB_WIDTH:(s + 1) * SB_WIDTH].reshape(SB_HEADS, SB_HEAD_DIM, tm)

    def head_norm(a, gain_ref):
        return a * lax.rsqrt(jnp.mean(a * a, axis=1, keepdims=True) + EPS) * gain_ref[...]

    qt_ref[...] = head_norm(heads(0), qn_ref)
    kt_ref[...] = head_norm(heads(1), kn_ref)
    vt_ref[...] = heads(2)


def _inproj(x, seq, depth, layer, prev_kv, g, w_t, gate_bias, qn, kn):
    n, d = x.shape
    tm = min(INPROJ_TM, seq)
    nt = seq // tm

    def row(c):
        return pl.BlockSpec((tm, c), lambda i: (i, 0))

    kv_spec = pl.BlockSpec((None, None, SB_HEADS, SB_HEAD_DIM, tm),
                           lambda i: (layer, i // nt, 0, 0, i % nt))
    kv_shape = jax.ShapeDtypeStruct((depth, n // seq, SB_HEADS, SB_HEAD_DIM, seq), F32)
    q_spec = pl.BlockSpec((None, SB_HEADS, SB_HEAD_DIM, tm), lambda i: (i // nt, 0, 0, i % nt))
    q_shape = jax.ShapeDtypeStruct((n // seq, SB_HEADS, SB_HEAD_DIM, seq), F32)
    widths = (N_GATES, SSD_WIDTH, SSD_CONV_DIM, LANES, 3 * SC_WIDTH)
    consts = (g, w_t, gate_bias, qn, kn)
    carried = [] if prev_kv is None else list(prev_kv)
    n_in = 1 + len(consts)
    return pl.pallas_call(
        _inproj_body,
        grid=(n // tm,),
        in_specs=[row(d)] + [_layer_spec(a, layer) for a in consts]
        + [pl.BlockSpec(memory_space=pl.ANY)] * len(carried),
        out_specs=[row(c) for c in widths] + [q_spec, kv_spec, kv_spec],
        out_shape=[jax.ShapeDtypeStruct((n, c), F32) for c in widths] + [q_shape, kv_shape, kv_shape],
        input_output_aliases={n_in: len(widths) + 1, n_in + 1: len(widths) + 2} if carried else {},
        compiler_params=_params("parallel"),
        name="inproj",
    )(x, *consts, *carried)


def _shift_rows(x, halo, k):
    r = pltpu.roll(x, k, axis=0)
    hk = pltpu.roll(halo, k, axis=0)
    rid = lax.broadcasted_iota(jnp.int32, hk.shape, 0)
    head = jnp.where(rid < k, hk, r[:SUBLANES])
    return jnp.concatenate([head, r[SUBLANES:]], axis=0)


def _causal_conv(x, halo, w):
    taps = w.shape[0]
    acc = _shift_rows(x, halo, taps - 1) * w[0:1]
    for k in range(1, taps - 1):
        acc = acc + _shift_rows(x, halo, taps - 1 - k) * w[k:k + 1]
    return acc + x * w[taps - 1:taps]


def _group_norm_store(dst_ref, y, nw):
    for g in range(SSD_GROUPS):
        sl = slice(g * SSD_GROUP_WIDTH, (g + 1) * SSD_GROUP_WIDTH)
        dst_ref[:, sl] = _rmsnorm(y[:, sl], nw[:, sl])


def _seqmix_body(z_ref, xbc_ref, xh_ref, dt_ref, sc_ref, sh_ref,
                 cw_ref, cb_ref, dtb_ref, alog_ref, dexp_ref, nw_ref, scw_ref, e_ref, tri_ref,
                 ya_ref, yb_ref, st_ref, cst_ref, scst_ref, hs_ref):
    t = pl.program_id(1)
    first = t == 0
    last = t == pl.num_programs(1) - 1
    L = xbc_ref.shape[0]

    @pl.when(first)
    def _():
        hs_ref[...] = jnp.zeros_like(hs_ref)

    xbc = xbc_ref[...]
    halo = jnp.where(first, 0.0, xh_ref[...])
    act = _silu(_causal_conv(xbc, halo, cw_ref[...]) + cb_ref[...])
    xs = act[:, :SSD_WIDTH]

    dt = _softplus(dt_ref[...] + dtb_ref[...])
    la = dt * (-jnp.exp(alog_ref[...]))
    cs = _dot_split_left(tri_ref[...], la, 3)
    cs_t = cs.T
    tot = cs[L - 1:L, :]
    to_end = jnp.exp(tot - cs)
    stack = jnp.concatenate(
        [dt, jnp.exp(cs), to_end, jnp.broadcast_to(jnp.exp(tot), (SUBLANES, LANES))], axis=0)
    ex = _dot_split(stack, e_ref[...], 3)
    dt_e, ecs_e, te_e, cd_e = ex[:L], ex[L:2 * L], ex[2 * L:3 * L], ex[3 * L:3 * L + 1]
    xd = xs * dt_e
    xd_b = xd.astype(BF16)
    xdte_b = (xd * te_e).astype(BF16)

    rid = lax.broadcasted_iota(jnp.int32, (L, L), 0)
    cid = lax.broadcasted_iota(jnp.int32, (L, L), 1)
    causal = rid >= cid
    low_half = lax.broadcasted_iota(jnp.int32, (L, LANES), 1) < SSD_HEAD_DIM

    y_groups = []
    for g in range(SSD_GROUPS):
        gs = slice(g * SSD_GROUP_WIDTH, (g + 1) * SSD_GROUP_WIDTH)
        b_g = act[:, SSD_WIDTH + g * SSD_STATE:SSD_WIDTH + (g + 1) * SSD_STATE]
        c0 = SSD_WIDTH + SSD_GROUPS * SSD_STATE
        c_g = act[:, c0 + g * SSD_STATE:c0 + (g + 1) * SSD_STATE]
        b_gb, c_gb = b_g.astype(BF16), c_g.astype(BF16)
        cb = _dot_nt(c_gb, b_gb)
        h_in = hs_ref[g]
        y_off = _dot(c_gb, h_in.astype(BF16)) * ecs_e[:, gs]
        pieces = []
        for pr in range(HEADS_PER_GROUP // 2):
            ha = g * HEADS_PER_GROUP + 2 * pr
            xp = xd_b[:, ha * SSD_HEAD_DIM:(ha + 2) * SSD_HEAD_DIM]
            yd = []
            for hh in (ha, ha + 1):
                seg = cs[:, hh:hh + 1] - cs_t[hh:hh + 1, :]
                m = cb * jnp.exp(jnp.where(causal, seg, -jnp.inf))
                yd.append(_dot(m.astype(BF16), xp))
            pieces.append(jnp.where(low_half, yd[0], yd[1]))
        y_groups.append(jnp.concatenate(pieces, axis=1) + y_off)
        s_new = _dot(b_g.T.astype(BF16), xdte_b[:, gs])
        hs_ref[g] = h_in * cd_e[:, gs] + s_new

    y = jnp.concatenate(y_groups, axis=1) + xs * dexp_ref[...]
    y = y * _silu(z_ref[...])
    _group_norm_store(ya_ref, y, nw_ref[...])

    sc = sc_ref[...]
    p = sc[:, SC_WIDTH:2 * SC_WIDTH] * sc[:, 2 * SC_WIDTH:]
    sh = sh_ref[...]
    ph = jnp.where(first, 0.0, sh[:, SC_WIDTH:2 * SC_WIDTH] * sh[:, 2 * SC_WIDTH:])
    yb_ref[...] = sc[:, :SC_WIDTH] * _causal_conv(p, ph, scw_ref[...])

    @pl.when(last)
    def _():
        for g in range(SSD_GROUPS):
            st_ref[g * HEADS_PER_GROUP:(g + 1) * HEADS_PER_GROUP] = hs_ref[g].T.reshape(
                HEADS_PER_GROUP, SSD_HEAD_DIM, SSD_STATE)
        cst_ref[...] = xbc[L - SUBLANES:]
        scst_ref[...] = p[L - SUBLANES:]


def _seqmix(z, xbc, dt, sc, layer, cw, cb, dtb, alog, dexp, nw, scw, e_mat, tri):
    b, t, _ = z.shape
    L = SSD_CHUNK
    hb = L // SUBLANES

    def cur(c):
        return pl.BlockSpec((None, L, c), lambda i, j: (i, j, 0))

    def halo(c):
        return pl.BlockSpec((None, SUBLANES, c), lambda i, j: (i, jnp.maximum(j * hb - 1, 0), 0))

    def per_seq(shape):
        return pl.BlockSpec((None,) + shape, lambda i, j: (i,) + (0,) * len(shape))

    consts = (cw, cb, dtb, alog, dexp, nw, scw, e_mat, tri)
    return pl.pallas_call(
        _seqmix_body,
        grid=(b, t // L),
        in_specs=[cur(SSD_WIDTH), cur(SSD_CONV_DIM), halo(SSD_CONV_DIM), cur(LANES),
                  cur(3 * SC_WIDTH), halo(3 * SC_WIDTH)]
        + [_layer_spec(a, layer) for a in consts[:-2]] + [_const_spec(a.shape) for a in consts[-2:]],
        out_specs=[cur(SSD_WIDTH), cur(SC_WIDTH), per_seq((SSD_HEADS, SSD_HEAD_DIM, SSD_STATE)),
                   per_seq((SUBLANES, SSD_CONV_DIM)), per_seq((SUBLANES, SC_WIDTH))],
        out_shape=[jax.ShapeDtypeStruct((b, t, SSD_WIDTH), F32),
                   jax.ShapeDtypeStruct((b, t, SC_WIDTH), F32),
                   jax.ShapeDtypeStruct((b, SSD_HEADS, SSD_HEAD_DIM, SSD_STATE), F32),
                   jax.ShapeDtypeStruct((b, SUBLANES, SSD_CONV_DIM), F32),
                   jax.ShapeDtypeStruct((b, SUBLANES, SC_WIDTH), F32)],
        scratch_shapes=[pltpu.VMEM((SSD_GROUPS, SSD_STATE, SSD_GROUP_WIDTH), F32)],
        compiler_params=_params("parallel", "arbitrary"),
        name="seqmix",
    )(z, xbc, xbc, dt, sc, sc, *consts)


def _sb_body(bias_ref, q_ref, k_ref, v_ref, u_ref, o_ref, *, layer):
    i = pl.program_id(1)
    tq = q_ref.shape[-1]
    u = u_ref[...]
    for hp in range(SB_HEADS // 2):
        _sb_head_pair(hp, i, tq, u, bias_ref, q_ref, k_ref, v_ref, o_ref, layer)


def _sb_head_pair(hp, i, tq, u, bias_ref, q_ref, k_ref, v_ref, o_ref, layer):
    heads = slice(2 * hp, 2 * hp + 2)
    q2 = q_ref[heads].reshape(LANES, tq).T
    low_half = lax.broadcasted_iota(jnp.int32, (tq, LANES), 1) < SB_HEAD_DIM
    q_heads = (jnp.where(low_half, q2, 0.0).astype(BF16), jnp.where(low_half, 0.0, q2).astype(BF16))
    biases = (bias_ref[layer, 2 * hp] * LOG2E, bias_ref[layer, 2 * hp + 1] * LOG2E)

    def blocks(js, carry, masks):
        acc, cs = carry[0], list(carry[1:])
        kv = []
        for j in js:
            off = pl.multiple_of(j * tq, tq)
            kv.append((k_ref[heads, :, pl.ds(off, tq)].reshape(LANES, tq).astype(BF16),
                       v_ref[heads, :, pl.ds(off, tq)].reshape(LANES, tq).astype(BF16)))
        chains = [(n, h) for n in range(len(js)) for h in range(2)]
        z2 = [_dot(q_heads[h], kv[n][0]) * (SB_SCALE * LOG2E) + biases[h] for n, h in chains]
        sp2, lb2 = [], []
        for (n, h), z in zip(chains, z2):
            sp = jnp.maximum(z, 0.0) + jnp.log2(1.0 + jnp.exp2(-jnp.abs(z)))
            lb2.append(z - sp)
            sp2.append(sp if masks[n] is None else jnp.where(masks[n], sp, 0.0))
        tails = [_dot(sp.astype(BF16), u) for sp in sp2]
        ws = []
        for (n, h), lb, tail, sp in zip(chains, lb2, tails, sp2):
            w = jnp.exp2(lb - (tail + cs[h]))
            ws.append(w if masks[n] is None else jnp.where(masks[n], w, 0.0))
            cs[h] = cs[h] + jnp.sum(sp, axis=1, keepdims=True)
        pv = [_dot_nt(w.astype(BF16), kv[n][1]) for (n, h), w in zip(chains, ws)]
        for n in range(len(js)):
            acc = acc + jnp.where(low_half, pv[2 * n], pv[2 * n + 1])
        return (acc, *cs)

    rid = lax.broadcasted_iota(jnp.int32, (tq, tq), 0)
    cid = lax.broadcasted_iota(jnp.int32, (tq, tq), 1)
    diag = cid < rid
    init = (jnp.zeros((tq, LANES), F32), jnp.zeros((tq, 1), F32), jnp.zeros((tq, 1), F32))
    carry = lax.cond(i == 0, lambda cr: blocks([i], cr, [diag]),
                     lambda cr: blocks([i, i - 1], cr, [diag, None]), init)
    rest = jnp.maximum(i - 1, 0)
    odd = rest % 2
    carry = lax.cond(odd == 1, lambda cr: blocks([i - 2], cr, [None]), lambda cr: cr, carry)
    top = i - 2 - odd
    acc, _, _ = lax.fori_loop(
        0, rest // 2,
        lambda p, cr: blocks([top - 2 * p, top - 2 * p - 1], cr, [None, None]), carry)
    o_ref[:, hp * LANES:(hp + 1) * LANES] = acc


def _sb_attn(q_t, k_t, v_t, layer, bias, u):
    b, _, _, t = q_t.shape
    w = SB_WIDTH
    tq = SB_TQ
    qspec = pl.BlockSpec((None, SB_HEADS, SB_HEAD_DIM, tq), lambda bi, i: (bi, 0, 0, i))
    ospec = pl.BlockSpec((None, tq, w), lambda bi, i: (bi, i, 0))
    kvspec = pl.BlockSpec((None, None, SB_HEADS, SB_HEAD_DIM, t), lambda bi, i: (layer, bi, 0, 0, 0))
    return pl.pallas_call(
        functools.partial(_sb_body, layer=layer),
        grid=(b, t // tq),
        in_specs=[pl.BlockSpec(memory_space=pltpu.SMEM), qspec, kvspec, kvspec,
                  _const_spec(u.shape)],
        out_specs=ospec,
        out_shape=jax.ShapeDtypeStruct((b, t, w), F32),
        compiler_params=_params("parallel", "arbitrary"),
        name="sb_attn",
    )(bias, q_t, k_t, v_t, u)


def _merge_ffn_body(x_ref, gates_ref, ya_ref, yb_ref, yc_ref, wa_ref, wb_ref, wc_ref, wo_ref,
                    g_ref, wg_ref, wu_ref, wd_ref, o_ref):
    gates = gates_ref[...]
    m = (gates[:, :D_MODEL] * _dot(ya_ref[...].astype(BF16), wa_ref[...])
         + gates[:, D_MODEL:2 * D_MODEL] * _dot(yb_ref[...].astype(BF16), wb_ref[...])
         + gates[:, 2 * D_MODEL:] * _dot(yc_ref[...].astype(BF16), wc_ref[...]))
    x2 = x_ref[...] + _dot(m.astype(BF16), wo_ref[...])
    o_ref[...] = _ffn_value(x2, g_ref, wg_ref, wu_ref, wd_ref)


def _merge_ffn(x, gates, ya, yb, yc, layer, wa, wb, wc, wo, g, wg, wu, wd):
    n, d = x.shape
    tm = min(MERGE_TM, n)

    def row(c):
        return pl.BlockSpec((tm, c), lambda i: (i, 0))

    consts = (wa, wb, wc, wo, g, wg, wu, wd)
    return pl.pallas_call(
        _merge_ffn_body,
        grid=(n // tm,),
        in_specs=[row(d), row(N_GATES), row(SSD_WIDTH), row(SC_WIDTH), row(SB_WIDTH)]
        + [_layer_spec(a, layer) for a in consts],
        out_specs=row(d),
        out_shape=jax.ShapeDtypeStruct((n, d), F32),
        compiler_params=_params("parallel"),
        name="merge_ffn",
    )(x, gates, ya, yb, yc, *consts)


def _dec_ssd_body(z_ref, xbc_ref, dt_ref, sc_ref, cst_ref, scst_ref, st_ref,
                  cw_ref, cb_ref, dtb_ref, alog_ref, dexp_ref, nw_ref, scw_ref, *rest):
    (ya_ref, yb_ref, ncst_ref, nscst_ref, nst_ref,
     xs_s, xdt_s, da_s, dab_s, bm_s, cm_s, xb_s, yt_s) = rest[-13:]
    h = pl.program_id(0)
    nb = z_ref.shape[0]

    @pl.when(h == 0)
    def _():
        xbc = xbc_ref[...]
        cw = cw_ref[...]
        conv = (cst_ref[0] * cw[0:1] + cst_ref[1] * cw[1:2] + cst_ref[2] * cw[2:3]
                + xbc * cw[3:4])
        ncst_ref[0] = cst_ref[1]
        ncst_ref[1] = cst_ref[2]
        ncst_ref[2] = xbc
        act = _silu(conv + cb_ref[...])
        xs = act[:, :SSD_WIDTH]
        xs_s[...] = xs
        c0 = SSD_WIDTH + SSD_GROUPS * SSD_STATE
        for g in range(SSD_GROUPS):
            bm_s[g] = act[:, SSD_WIDTH + g * SSD_STATE:SSD_WIDTH + (g + 1) * SSD_STATE]
            cm_s[g] = act[:, c0 + g * SSD_STATE:c0 + (g + 1) * SSD_STATE]
        dt = _softplus(dt_ref[...] + dtb_ref[...])
        da_s[...] = jnp.exp(dt * (-jnp.exp(alog_ref[...])))
        dt_t = dt.T
        xs_t = xs.T
        for hh in range(SSD_HEADS):
            xdt_s[hh] = xs_t[hh * SSD_HEAD_DIM:(hh + 1) * SSD_HEAD_DIM] * dt_t[hh:hh + 1, :]
        sc = sc_ref[...]
        p = sc[:, SC_WIDTH:2 * SC_WIDTH] * sc[:, 2 * SC_WIDTH:]
        scw = scw_ref[...]
        yb_ref[...] = sc[:, :SC_WIDTH] * (
            scst_ref[0] * scw[0:1] + scst_ref[1] * scw[1:2] + p * scw[2:3])
        nscst_ref[0] = scst_ref[1]
        nscst_ref[1] = p

    lane_t = lax.broadcasted_iota(jnp.int32, (nb, LANES), 1)
    da_col = jnp.sum(jnp.where(lane_t == h, da_s[...], 0.0), axis=1, keepdims=True)
    dab_s[...] = jnp.broadcast_to(da_col, (nb, LANES))

    xt = xdt_s[h]
    for b in range(nb):
        xb_s[b] = jnp.broadcast_to(xt[:, b:b + 1], (SSD_HEAD_DIM, LANES))

    g = h // HEADS_PER_GROUP
    lane = lax.broadcasted_iota(jnp.int32, (SSD_HEAD_DIM, LANES), 1)
    yacc = jnp.zeros((SSD_HEAD_DIM, LANES), F32)
    for b in range(nb):
        hn = st_ref[b] * dab_s[b:b + 1, :] + xb_s[b] * bm_s[g, b:b + 1, :]
        nst_ref[b] = hn
        ycol = jnp.sum(hn * cm_s[g, b:b + 1, :], axis=1, keepdims=True)
        yacc = jnp.where(lane == b, ycol, yacc)
    yt_s[h] = yacc

    @pl.when(h == pl.num_programs(0) - 1)
    def _():
        xs = xs_s[...]
        y = yt_s[...].reshape(SSD_WIDTH, nb).T + xs * dexp_ref[...]
        y = y * _silu(z_ref[...])
        _group_norm_store(ya_ref, y, nw_ref[...])


def _dec_ssd(z, xbc, dt, sc, cst, scst, state, layer, prev_state, cw, cb, dtb, alog, dexp, nw, scw):
    nb = z.shape[0]
    consts = (cw, cb, dtb, alog, dexp, nw, scw)
    st_spec = pl.BlockSpec((None, nb, None, SSD_HEAD_DIM, SSD_STATE), lambda h: (layer, 0, h, 0, 0))
    cst_shape, scst_shape = cst.shape[1:], scst.shape[1:]
    carried = [] if prev_state is None else [prev_state]
    n_in = 7 + len(consts)
    return pl.pallas_call(
        _dec_ssd_body,
        grid=(SSD_HEADS,),
        in_specs=[_const_spec(a.shape) for a in (z, xbc, dt, sc)]
        + [_layer_spec(a, layer) for a in (cst, scst)] + [st_spec]
        + [_layer_spec(a, layer) for a in consts] + [pl.BlockSpec(memory_space=pl.ANY)] * len(carried),
        out_specs=[pl.BlockSpec((nb, SSD_WIDTH), lambda h: (0, 0)),
                   pl.BlockSpec((nb, SC_WIDTH), lambda h: (0, 0)),
                   pl.BlockSpec(cst_shape, lambda h: (0, 0, 0)),
                   pl.BlockSpec(scst_shape, lambda h: (0, 0, 0)),
                   st_spec],
        out_shape=[jax.ShapeDtypeStruct((nb, SSD_WIDTH), F32),
                   jax.ShapeDtypeStruct((nb, SC_WIDTH), F32),
                   jax.ShapeDtypeStruct(cst_shape, F32),
                   jax.ShapeDtypeStruct(scst_shape, F32),
                   jax.ShapeDtypeStruct(state.shape, F32)],
        scratch_shapes=[pltpu.VMEM((nb, SSD_WIDTH), F32),
                        pltpu.VMEM((SSD_HEADS, SSD_HEAD_DIM, nb), F32),
                        pltpu.VMEM((nb, LANES), F32),
                        pltpu.VMEM((nb, LANES), F32),
                        pltpu.VMEM((SSD_GROUPS, nb, SSD_STATE), F32),
                        pltpu.VMEM((SSD_GROUPS, nb, SSD_STATE), F32),
                        pltpu.VMEM((nb, SSD_HEAD_DIM, LANES), F32),
                        pltpu.VMEM((SSD_HEADS, SSD_HEAD_DIM, nb), F32)],
        input_output_aliases={n_in: 4} if carried else {},
        compiler_params=_params("arbitrary"),
        name="dec_ssd",
    )(z, xbc, dt, sc, cst, scst, state, *consts, *carried)


def _ffn_attn_body(pt_ref, x_ref, g_ref, wg_ref, wu_ref, wd_ref, qt_ref, bias_ref, u_ref, p_ref,
                   *refs):
    n_pages = (len(refs) - 3) // 2
    k_refs, v_refs = refs[:n_pages], refs[n_pages:2 * n_pages]
    o_ref, yo_ref, w_s = refs[2 * n_pages:]
    b = pl.program_id(0)
    nb = qt_ref.shape[1]

    @pl.when(b == 0)
    def _():
        yo_ref[...] = jnp.zeros_like(yo_ref)

    token = lax.broadcasted_iota(jnp.int32, (SB_WIDTH, nb), 1) == b
    q_col = jnp.sum(jnp.where(token, qt_ref[...], 0.0), axis=1, keepdims=True)
    sub = lax.broadcasted_iota(jnp.int32, (SB_HEADS, LANES), 0)

    def scores(p):
        zp = jnp.zeros((SB_HEADS, LANES), F32)
        for h in range(SB_HEADS):
            prod = k_refs[p][h] * q_col[h * SB_HEAD_DIM:(h + 1) * SB_HEAD_DIM]
            zp = jnp.where(sub == h, jnp.sum(prod, axis=0, keepdims=True), zp)
        return zp

    def weights(z_pages):
        z = jnp.concatenate(z_pages, axis=0) * SB_SCALE + bias_ref[...]
        sp = _softplus(z)
        lk = -sp
        n = z.shape[0]
        tot = jnp.broadcast_to(jnp.sum(lk, axis=1, keepdims=True), (n, LANES))
        tail = _dot_split(lk, u_ref[...], 3) + _dot_split_left(p_ref[...], tot, 3)
        w_s[...] = jnp.exp((z - sp) + tail)

    def values(h):
        acc = jnp.zeros((SB_HEAD_DIM, LANES), F32)
        for p in range(n_pages):
            r = p * SB_HEADS + h
            acc = acc + v_refs[p][h] * w_s[r:r + 1, :]
        return jnp.sum(acc, axis=1, keepdims=True)

    weights([scores(p) for p in range(n_pages)])
    o_col = jnp.concatenate([values(h) for h in range(SB_HEADS)], axis=0)
    yo_ref[...] = jnp.where(token, o_col, yo_ref[...])
    _ffn_body(x_ref, g_ref, wg_ref, wu_ref, wd_ref, o_ref)


def _ffn_attn(x, layer, g, wg, wu, wd, page_table, q_t, cache_kt, cache_vt, bias_col, u, pmat):
    n, d = x.shape
    nb, n_pages = page_table.shape
    page = cache_kt.shape[-1]
    tm = n // nb
    assert tm * nb == n and tm % SUBLANES == 0

    def page_spec(j):
        return pl.BlockSpec((None, None, SB_HEADS, SB_HEAD_DIM, page),
                            lambda b, pt: (layer, pt[b, j], 0, 0, 0))

    def const(a):
        return pl.BlockSpec(a.shape, lambda b, pt: (0,) * a.ndim, pipeline_mode=pl.Buffered(1))

    row = pl.BlockSpec((tm, d), lambda b, pt: (b, 0))
    grid_spec = pltpu.PrefetchScalarGridSpec(
        num_scalar_prefetch=1,
        grid=(nb,),
        in_specs=[row] + [_layer_spec(a, layer) for a in (g, wg, wu, wd)] + [const(q_t)]
        + [_layer_spec(bias_col, layer), const(u), const(pmat)]
        + [page_spec(j) for j in range(n_pages)] * 2,
        out_specs=[row, pl.BlockSpec((SB_WIDTH, nb), lambda b, pt: (0, 0))],
        scratch_shapes=[pltpu.VMEM((n_pages * SB_HEADS, LANES), F32)],
    )
    return pl.pallas_call(
        _ffn_attn_body,
        grid_spec=grid_spec,
        out_shape=[jax.ShapeDtypeStruct((n, d), F32), jax.ShapeDtypeStruct((SB_WIDTH, nb), F32)],
        compiler_params=_params("arbitrary"),
        name="ffn_attn",
    )(page_table, x, g, wg, wu, wd, q_t, bias_col, u, pmat,
      *([cache_kt] * n_pages), *([cache_vt] * n_pages))


def _tri01(n, strict_upper_rows):
    r = lax.broadcasted_iota(jnp.int32, (n, n), 0)
    c = lax.broadcasted_iota(jnp.int32, (n, n), 1)
    return ((r > c) if strict_upper_rows else (r >= c)).astype(BF16)


def kernel(x_prompt, x_sample, cache_sb_k, cache_sb_v, page_table, state_ssd, state_ssd_conv, state_sc_conv, ffn1_norm, ffn1_w_gate, ffn1_w_up, ffn1_w_down, mix_norm, w_in, gate_bias, ssd_conv_w, ssd_conv_b, ssd_dt_bias, ssd_a_log, ssd_d, ssd_norm, sc_conv_w, sb_q_norm, sb_k_norm, sb_logit_bias, w_proj_a, w_proj_b, w_proj_c, w_out, ffn2_norm, ffn2_w_gate, ffn2_w_up, ffn2_w_down):
    depth = w_in.shape[0]
    bsz, seq, d = x_prompt.shape
    dec_b = x_sample.shape[0]
    page = cache_sb_k.shape[2]
    n_pages = page_table.shape[1]
    cache_kt = cache_sb_k.transpose(0, 1, 3, 4, 2)
    cache_vt = cache_sb_v.transpose(0, 1, 3, 4, 2)

    tri_incl = _tri01(SSD_CHUNK, False)
    u_tq = _tri01(SB_TQ, True)
    u_page = _tri01(page, True)
    lane = jnp.arange(LANES)
    col = jnp.arange(SSD_WIDTH)
    e_mat = (lane[:, None] == (col[None, :] // SSD_HEAD_DIM)).astype(BF16)
    r = jnp.arange(n_pages * SB_HEADS)
    pmat = ((r[:, None] % SB_HEADS == r[None, :] % SB_HEADS)
            & (r[None, :] // SB_HEADS > r[:, None] // SB_HEADS)).astype(BF16)

    def pad_lanes(a):
        return jnp.pad(a, ((0, 0), (0, LANES - a.shape[1])))

    row3 = lambda a: a[:, None, :]
    w_in_t = w_in.transpose(0, 2, 1).astype(BF16)
    f1 = (row3(ffn1_norm), ffn1_w_gate.astype(BF16), ffn1_w_up.astype(BF16), ffn1_w_down.astype(BF16))
    f2 = (row3(ffn2_norm), ffn2_w_gate.astype(BF16), ffn2_w_up.astype(BF16), ffn2_w_down.astype(BF16))
    ip = (row3(mix_norm), w_in_t, row3(gate_bias), sb_q_norm[:, :, None],
          sb_k_norm[:, :, None])
    ssd_p = (ssd_conv_w, row3(ssd_conv_b), row3(pad_lanes(ssd_dt_bias)), row3(pad_lanes(ssd_a_log)),
             row3(jnp.repeat(ssd_d, SSD_HEAD_DIM, axis=1)), row3(ssd_norm), sc_conv_w)
    mg = (w_proj_a.astype(BF16), w_proj_b.astype(BF16), w_proj_c.astype(BF16), w_out.astype(BF16))
    bias_col = jnp.tile(sb_logit_bias, (1, n_pages))[:, :, None]
    cst_in = state_ssd_conv.transpose(0, 2, 1, 3)
    scst_in = state_sc_conv.transpose(0, 2, 1, 3)

    xp = x_prompt.reshape(bsz * seq, d)
    xs = x_sample.reshape(dec_b * x_sample.shape[1], d)
    kv_p = kv_s = nst = None
    hst, cst, scst, ncst, nscst = [], [], [], [], []
    as_seq = lambda a: a.reshape(bsz, seq, a.shape[-1])
    for l in range(depth):
        s1 = _ffn(xs, l, *f1)
        s_gates, s_z, s_xbc, s_dt, s_sc, s_qt, *kv_s = _inproj(s1, dec_b, depth, l, kv_s, *ip)

        x1, yc_t = _ffn_attn(xp, l, *f1, page_table, s_qt.reshape(SB_WIDTH, dec_b), cache_kt, cache_vt,
                             bias_col, u_page, pmat)
        gates, z, xbc, dt, sc, q_t, *kv_p = _inproj(x1, seq, depth, l, kv_p, *ip)
        ya, yb, h_l, cst8, scst8 = _seqmix(as_seq(z), as_seq(xbc), as_seq(dt), as_seq(sc), l,
                                            *ssd_p, e_mat, tri_incl)
        yc = _sb_attn(q_t, *kv_p, l, sb_logit_bias, u_tq)
        xp = _merge_ffn(x1, gates, ya.reshape(-1, SSD_WIDTH), yb.reshape(-1, SC_WIDTH),
                        yc.reshape(-1, SB_WIDTH), l, *mg, *f2)
        hst.append(h_l)
        cst.append(cst8[:, SUBLANES - (SSD_CONV - 1):])
        scst.append(scst8[:, SUBLANES - (SC_CONV - 1):])

        ya, yb, ncst_l, nscst_l, nst = _dec_ssd(s_z, s_xbc, s_dt, s_sc, cst_in, scst_in,
                                                state_ssd, l, nst, *ssd_p)
        xs = _merge_ffn(s1, s_gates, ya, yb, yc_t.T, l, *mg, *f2)
        ncst.append(ncst_l)
        nscst.append(nscst_l)

    kp, vp = (a.transpose(0, 1, 4, 2, 3) for a in kv_p)
    ks, vs = (a.transpose(0, 4, 1, 2, 3) for a in kv_s)
    return (xp.reshape(bsz, seq, d), xs.reshape(dec_b, x_sample.shape[1], d),
            kp, vp, ks, vs, jnp.stack(hst), nst, jnp.stack(cst),
            jnp.stack(ncst).transpose(0, 2, 1, 3), jnp.stack(scst),
            jnp.stack(nscst).transpose(0, 2, 1, 3))
```

```python
import functools

import jax
import jax.numpy as jnp
from jax import lax
from jax.experimental import pallas as pl
from jax.experimental.pallas import tpu as pltpu

F32 = jnp.float32
BF16 = jnp.bfloat16
EPS = 1e-6

LANES = 128
SUBLANES = 8
VMEM_LIMIT = 56 * 1024 * 1024

D_MODEL = 1024
SSD_HEADS = 16
SSD_HEAD_DIM = 64
SSD_WIDTH = SSD_HEADS * SSD_HEAD_DIM
SSD_GROUPS = 2
SSD_STATE = 128
SSD_CONV = 4
SSD_CONV_DIM = SSD_WIDTH + 2 * SSD_GROUPS * SSD_STATE
SSD_GROUP_WIDTH = SSD_WIDTH // SSD_GROUPS
HEADS_PER_GROUP = SSD_HEADS // SSD_GROUPS
SC_WIDTH = 512
SC_CONV = 3
SB_HEADS = 8
SB_HEAD_DIM = 64
SB_WIDTH = SB_HEADS * SB_HEAD_DIM
SB_SCALE = SB_HEAD_DIM ** -0.5
LOG2E = 1.4426950408889634
N_GATES = 3 * D_MODEL

R_GATE = 0
R_Z = R_GATE + N_GATES
R_XBC = R_Z + SSD_WIDTH
R_DT = R_XBC + SSD_CONV_DIM
R_SC = R_DT + SSD_HEADS
R_QKV = R_SC + 3 * SC_WIDTH
R_END = R_QKV + 3 * SB_WIDTH

FFN_TM = 512
FFN_CHUNKS = 1
INPROJ_TM = 256
MERGE_TM = 256
SSD_CHUNK = 128
SB_TQ = 256


def _dot(a, b):
    return jnp.dot(a, b, preferred_element_type=F32)


def _dot_nt(a, b):
    return lax.dot_general(a, b, (((1,), (1,)), ((), ())), preferred_element_type=F32)


def _bf16_pieces(a, n):
    pieces, r = [], a
    for i in range(n):
        p = r.astype(BF16)
        pieces.append(p)
        if i + 1 < n:
            r = r - p.astype(F32)
    return pieces


def _dot_split(a, b01, n):
    out = None
    for p in _bf16_pieces(a, n):
        d = _dot(p, b01)
        out = d if out is None else out + d
    return out


def _dot_split_left(a01, b, n):
    out = None
    for p in _bf16_pieces(b, n):
        d = _dot(a01, p)
        out = d if out is None else out + d
    return out


def _rmsnorm(x, g):
    return x * lax.rsqrt(jnp.mean(x * x, axis=-1, keepdims=True) + EPS) * g


def _silu(x):
    return x * jax.nn.sigmoid(x)


def _softplus(x):
    return jnp.maximum(x, 0.0) + jnp.log1p(jnp.exp(-jnp.abs(x)))


def _const_spec(shape):
    nd = len(shape)
    return pl.BlockSpec(shape, lambda *_: (0,) * nd, pipeline_mode=pl.Buffered(1))


def _layer_spec(a, layer):
    nd = a.ndim - 1
    return pl.BlockSpec((None,) + a.shape[1:], lambda *_: (layer,) + (0,) * nd,
                        pipeline_mode=pl.Buffered(1))


def _params(*sem):
    return pltpu.CompilerParams(dimension_semantics=sem, vmem_limit_bytes=VMEM_LIMIT)


def _ffn_value(x, g_ref, wg_ref, wu_ref, wd_ref):
    h = _rmsnorm(x, g_ref[...]).astype(BF16)
    c = wg_ref.shape[1] // FFN_CHUNKS
    y = None
    for i in range(FFN_CHUNKS):
        a = _dot(h, wg_ref[:, i * c:(i + 1) * c])
        u = _dot(h, wu_ref[:, i * c:(i + 1) * c])
        act = (_silu(a) * u).astype(BF16)
        yi = _dot(act, wd_ref[i * c:(i + 1) * c, :])
        y = yi if y is None else y + yi
    return x + 0.5 * y


def _ffn_body(x_ref, g_ref, wg_ref, wu_ref, wd_ref, o_ref):
    o_ref[...] = _ffn_value(x_ref[...], g_ref, wg_ref, wu_ref, wd_ref)


def _ffn(x, layer, g, wg, wu, wd):
    n, d = x.shape
    tm = min(FFN_TM, n)
    row = pl.BlockSpec((tm, d), lambda i: (i, 0))
    return pl.pallas_call(
        _ffn_body,
        grid=(n // tm,),
        in_specs=[row] + [_layer_spec(a, layer) for a in (g, wg, wu, wd)],
        out_specs=row,
        out_shape=jax.ShapeDtypeStruct((n, d), F32),
        compiler_params=_params("parallel"),
        name="ffn",
    )(x, g, wg, wu, wd)


def _inproj_body(x_ref, g_ref, w_ref, gb_ref, qn_ref, kn_ref, *rest):
    gates_ref, z_ref, xbc_ref, dt_ref, sc_ref, qt_ref, kt_ref, vt_ref = rest[-8:]
    h = _rmsnorm(x_ref[...], g_ref[...]).astype(BF16)

    def seg(lo, hi):
        return _dot_nt(h, w_ref[lo:hi, :])

    gates_ref[...] = jax.nn.sigmoid(seg(R_GATE, R_Z) + gb_ref[...])
    z_ref[...] = seg(R_Z, R_XBC)
    xbc_ref[...] = seg(R_XBC, R_DT)
    sc_ref[...] = seg(R_SC, R_QKV)
    dt_ref[...] = jnp.zeros_like(dt_ref)
    dt_ref[:, :SSD_HEADS] = seg(R_DT, R_SC)

    qkv_t = _dot_nt(w_ref[R_QKV:R_END, :], h)
    tm = qkv_t.shape[1]

    def heads(s):
        return qkv_t[s * SB_WIDTH:(s + 1) * SB_WIDTH].reshape(SB_HEADS, SB_HEAD_DIM, tm)

    def head_norm(a, gain_ref):
        return a * lax.rsqrt(jnp.mean(a * a, axis=1, keepdims=True) + EPS) * gain_ref[...]

    qt_ref[...] = head_norm(heads(0), qn_ref)
    kt_ref[...] = head_norm(heads(1), kn_ref)
    vt_ref[...] = heads(2)


def _inproj(x, seq, depth, layer, prev_kv, g, w_t, gate_bias, qn, kn):
    n, d = x.shape
    tm = min(INPROJ_TM, seq)
    nt = seq // tm

    def row(c):
        return pl.BlockSpec((tm, c), lambda i: (i, 0))

    kv_spec = pl.BlockSpec((None, None, SB_HEADS, SB_HEAD_DIM, tm),
                           lambda i: (layer, i // nt, 0, 0, i % nt))
    kv_shape = jax.ShapeDtypeStruct((depth, n // seq, SB_HEADS, SB_HEAD_DIM, seq), F32)
    q_spec = pl.BlockSpec((None, SB_HEADS, SB_HEAD_DIM, tm), lambda i: (i // nt, 0, 0, i % nt))
    q_shape = jax.ShapeDtypeStruct((n // seq, SB_HEADS, SB_HEAD_DIM, seq), F32)
    widths = (N_GATES, SSD_WIDTH, SSD_CONV_DIM, LANES, 3 * SC_WIDTH)
    consts = (g, w_t, gate_bias, qn, kn)
    carried = [] if prev_kv is None else list(prev_kv)
    n_in = 1 + len(consts)
    return pl.pallas_call(
        _inproj_body,
        grid=(n // tm,),
        in_specs=[row(d)] + [_layer_spec(a, layer) for a in consts]
        + [pl.BlockSpec(memory_space=pl.ANY)] * len(carried),
        out_specs=[row(c) for c in widths] + [q_spec, kv_spec, kv_spec],
        out_shape=[jax.ShapeDtypeStruct((n, c), F32) for c in widths] + [q_shape, kv_shape, kv_shape],
        input_output_aliases={n_in: len(widths) + 1, n_in + 1: len(widths) + 2} if carried else {},
        compiler_params=_params("parallel"),
        name="inproj",
    )(x, *consts, *carried)


def _shift_rows(x, halo, k):
    r = pltpu.roll(x, k, axis=0)
    hk = pltpu.roll(halo, k, axis=0)
    rid = lax.broadcasted_iota(jnp.int32, hk.shape, 0)
    head = jnp.where(rid < k, hk, r[:SUBLANES])
    return jnp.concatenate([head, r[SUBLANES:]], axis=0)


def _causal_conv(x, halo, w):
    taps = w.shape[0]
    acc = _shift_rows(x, halo, taps - 1) * w[0:1]
    for k in range(1, taps - 1):
        acc = acc + _shift_rows(x, halo, taps - 1 - k) * w[k:k + 1]
    return acc + x * w[taps - 1:taps]


def _group_norm_store(dst_ref, y, nw):
    for g in range(SSD_GROUPS):
        sl = slice(g * SSD_GROUP_WIDTH, (g + 1) * SSD_GROUP_WIDTH)
        dst_ref[:, sl] = _rmsnorm(y[:, sl], nw[:, sl])


def _seqmix_body(z_ref, xbc_ref, xh_ref, dt_ref, sc_ref, sh_ref,
                 cw_ref, cb_ref, dtb_ref, alog_ref, dexp_ref, nw_ref, scw_ref, e_ref, tri_ref,
                 ya_ref, yb_ref, st_ref, cst_ref, scst_ref, hs_ref):
    t = pl.program_id(1)
    first = t == 0
    last = t == pl.num_programs(1) - 1
    L = xbc_ref.shape[0]

    @pl.when(first)
    def _():
        hs_ref[...] = jnp.zeros_like(hs_ref)

    xbc = xbc_ref[...]
    halo = jnp.where(first, 0.0, xh_ref[...])
    act = _silu(_causal_conv(xbc, halo, cw_ref[...]) + cb_ref[...])
    xs = act[:, :SSD_WIDTH]

    dt = _softplus(dt_ref[...] + dtb_ref[...])
    la = dt * (-jnp.exp(alog_ref[...]))
    cs = _dot_split_left(tri_ref[...], la, 3)
    cs_t = cs.T
    tot = cs[L - 1:L, :]
    to_end = jnp.exp(tot - cs)
    stack = jnp.concatenate(
        [dt, jnp.exp(cs), to_end, jnp.broadcast_to(jnp.exp(tot), (SUBLANES, LANES))], axis=0)
    ex = _dot_split(stack, e_ref[...], 3)
    dt_e, ecs_e, te_e, cd_e = ex[:L], ex[L:2 * L], ex[2 * L:3 * L], ex[3 * L:3 * L + 1]
    xd = xs * dt_e
    xd_b = xd.astype(BF16)
    xdte_b = (xd * te_e).astype(BF16)

    rid = lax.broadcasted_iota(jnp.int32, (L, L), 0)
    cid = lax.broadcasted_iota(jnp.int32, (L, L), 1)
    causal = rid >= cid
    low_half = lax.broadcasted_iota(jnp.int32, (L, LANES), 1) < SSD_HEAD_DIM

    y_groups = []
    for g in range(SSD_GROUPS):
        gs = slice(g * SSD_GROUP_WIDTH, (g + 1) * SSD_GROUP_WIDTH)
        b_g = act[:, SSD_WIDTH + g * SSD_STATE:SSD_WIDTH + (g + 1) * SSD_STATE]
        c0 = SSD_WIDTH + SSD_GROUPS * SSD_STATE
        c_g = act[:, c0 + g * SSD_STATE:c0 + (g + 1) * SSD_STATE]
        b_gb, c_gb = b_g.astype(BF16), c_g.astype(BF16)
        cb = _dot_nt(c_gb, b_gb)
        h_in = hs_ref[g]
        y_off = _dot(c_gb, h_in.astype(BF16)) * ecs_e[:, gs]
        pieces = []
        for pr in range(HEADS_PER_GROUP // 2):
            ha = g * HEADS_PER_GROUP + 2 * pr
            xp = xd_b[:, ha * SSD_HEAD_DIM:(ha + 2) * SSD_HEAD_DIM]
            yd = []
            for hh in (ha, ha + 1):
                seg = cs[:, hh:hh + 1] - cs_t[hh:hh + 1, :]
                m = cb * jnp.exp(jnp.where(causal, seg, -jnp.inf))
                yd.append(_dot(m.astype(BF16), xp))
            pieces.append(jnp.where(low_half, yd[0], yd[1]))
        y_groups.append(jnp.concatenate(pieces, axis=1) + y_off)
        s_new = _dot(b_g.T.astype(BF16), xdte_b[:, gs])
        hs_ref[g] = h_in * cd_e[:, gs] + s_new

    y = jnp.concatenate(y_groups, axis=1) + xs * dexp_ref[...]
    y = y * _silu(z_ref[...])
    _group_norm_store(ya_ref, y, nw_ref[...])

    sc = sc_ref[...]
    p = sc[:, SC_WIDTH:2 * SC_WIDTH] * sc[:, 2 * SC_WIDTH:]
    sh = sh_ref[...]
    ph = jnp.where(first, 0.0, sh[:, SC_WIDTH:2 * SC_WIDTH] * sh[:, 2 * SC_WIDTH:])
    yb_ref[...] = sc[:, :SC_WIDTH] * _causal_conv(p, ph, scw_ref[...])

    @pl.when(last)
    def _():
        for g in range(SSD_GROUPS):
            st_ref[g * HEADS_PER_GROUP:(g + 1) * HEADS_PER_GROUP] = hs_ref[g].T.reshape(
                HEADS_PER_GROUP, SSD_HEAD_DIM, SSD_STATE)
        cst_ref[...] = xbc[L - SUBLANES:]
        scst_ref[...] = p[L - SUBLANES:]


def _seqmix(z, xbc, dt, sc, layer, cw, cb, dtb, alog, dexp, nw, scw, e_mat, tri):
    b, t, _ = z.shape
    L = SSD_CHUNK
    hb = L // SUBLANES

    def cur(c):
        return pl.BlockSpec((None, L, c), lambda i, j: (i, j, 0))

    def halo(c):
        return pl.BlockSpec((None, SUBLANES, c), lambda i, j: (i, jnp.maximum(j * hb - 1, 0), 0))

    def per_seq(shape):
        return pl.BlockSpec((None,) + shape, lambda i, j: (i,) + (0,) * len(shape))

    consts = (cw, cb, dtb, alog, dexp, nw, scw, e_mat, tri)
    return pl.pallas_call(
        _seqmix_body,
        grid=(b, t // L),
        in_specs=[cur(SSD_WIDTH), cur(SSD_CONV_DIM), halo(SSD_CONV_DIM), cur(LANES),
                  cur(3 * SC_WIDTH), halo(3 * SC_WIDTH)]
        + [_layer_spec(a, layer) for a in consts[:-2]] + [_const_spec(a.shape) for a in consts[-2:]],
        out_specs=[cur(SSD_WIDTH), cur(SC_WIDTH), per_seq((SSD_HEADS, SSD_HEAD_DIM, SSD_STATE)),
                   per_seq((SUBLANES, SSD_CONV_DIM)), per_seq((SUBLANES, SC_WIDTH))],
        out_shape=[jax.ShapeDtypeStruct((b, t, SSD_WIDTH), F32),
                   jax.ShapeDtypeStruct((b, t, SC_WIDTH), F32),
                   jax.ShapeDtypeStruct((b, SSD_HEADS, SSD_HEAD_DIM, SSD_STATE), F32),
                   jax.ShapeDtypeStruct((b, SUBLANES, SSD_CONV_DIM), F32),
                   jax.ShapeDtypeStruct((b, SUBLANES, SC_WIDTH), F32)],
        scratch_shapes=[pltpu.VMEM((SSD_GROUPS, SSD_STATE, SSD_GROUP_WIDTH), F32)],
        compiler_params=_params("parallel", "arbitrary"),
        name="seqmix",
    )(z, xbc, xbc, dt, sc, sc, *consts)


def _sb_body(bias_ref, q_ref, k_ref, v_ref, u_ref, o_ref, *, layer):
    i = pl.program_id(1)
    tq = q_ref.shape[-1]
    u = u_ref[...]
    for hp in range(SB_HEADS // 2):
        _sb_head_pair(hp, i, tq, u, bias_ref, q_ref, k_ref, v_ref, o_ref, layer)


def _sb_head_pair(hp, i, tq, u, bias_ref, q_ref, k_ref, v_ref, o_ref, layer):
    heads = slice(2 * hp, 2 * hp + 2)
    q2 = q_ref[heads].reshape(LANES, tq).T
    low_half = lax.broadcasted_iota(jnp.int32, (tq, LANES), 1) < SB_HEAD_DIM
    q_heads = (jnp.where(low_half, q2, 0.0).astype(BF16), jnp.where(low_half, 0.0, q2).astype(BF16))
    biases = (bias_ref[layer, 2 * hp] * LOG2E, bias_ref[layer, 2 * hp + 1] * LOG2E)

    def blocks(js, carry, masks):
        acc, cs = carry[0], list(carry[1:])
        kv = []
        for j in js:
            off = pl.multiple_of(j * tq, tq)
            kv.append((k_ref[heads, :, pl.ds(off, tq)].reshape(LANES, tq).astype(BF16),
                       v_ref[heads, :, pl.ds(off, tq)].reshape(LANES, tq).astype(BF16)))
        chains = [(n, h) for n in range(len(js)) for h in range(2)]
        z2 = [_dot(q_heads[h], kv[n][0]) * (SB_SCALE * LOG2E) + biases[h] for n, h in chains]
        sp2, lb2 = [], []
        for (n, h), z in zip(chains, z2):
            sp = jnp.maximum(z, 0.0) + jnp.log2(1.0 + jnp.exp2(-jnp.abs(z)))
            lb2.append(z - sp)
            sp2.append(sp if masks[n] is None else jnp.where(masks[n], sp, 0.0))
        tails = [_dot(sp.astype(BF16), u) for sp in sp2]
        ws = []
        for (n, h), lb, tail, sp in zip(chains, lb2, tails, sp2):
            w = jnp.exp2(lb - (tail + cs[h]))
            ws.append(w if masks[n] is None else jnp.where(masks[n], w, 0.0))
            cs[h] = cs[h] + jnp.sum(sp, axis=1, keepdims=True)
        pv = [_dot_nt(w.astype(BF16), kv[n][1]) for (n, h), w in zip(chains, ws)]
        for n in range(len(js)):
            acc = acc + jnp.where(low_half, pv[2 * n], pv[2 * n + 1])
        return (acc, *cs)

    rid = lax.broadcasted_iota(jnp.int32, (tq, tq), 0)
    cid = lax.broadcasted_iota(jnp.int32, (tq, tq), 1)
    diag = cid < rid
    init = (jnp.zeros((tq, LANES), F32), jnp.zeros((tq, 1), F32), jnp.zeros((tq, 1), F32))
    carry = lax.cond(i == 0, lambda cr: blocks([i], cr, [diag]),
                     lambda cr: blocks([i, i - 1], cr, [diag, None]), init)
    rest = jnp.maximum(i - 1, 0)
    odd = rest % 2
    carry = lax.cond(odd == 1, lambda cr: blocks([i - 2], cr, [None]), lambda cr: cr, carry)
    top = i - 2 - odd
    acc, _, _ = lax.fori_loop(
        0, rest // 2,
        lambda p, cr: blocks([top - 2 * p, top - 2 * p - 1], cr, [None, None]), carry)
    o_ref[:, hp * LANES:(hp + 1) * LANES] = acc


def _sb_attn(q_t, k_t, v_t, layer, bias, u):
    b, _, _, t = q_t.shape
    w = SB_WIDTH
    tq = SB_TQ
    qspec = pl.BlockSpec((None, SB_HEADS, SB_HEAD_DIM, tq), lambda bi, i: (bi, 0, 0, i))
    ospec = pl.BlockSpec((None, tq, w), lambda bi, i: (bi, i, 0))
    kvspec = pl.BlockSpec((None, None, SB_HEADS, SB_HEAD_DIM, t), lambda bi, i: (layer, bi, 0, 0, 0))
    return pl.pallas_call(
        functools.partial(_sb_body, layer=layer),
        grid=(b, t // tq),
        in_specs=[pl.BlockSpec(memory_space=pltpu.SMEM), qspec, kvspec, kvspec,
                  _const_spec(u.shape)],
        out_specs=ospec,
        out_shape=jax.ShapeDtypeStruct((b, t, w), F32),
        compiler_params=_params("parallel", "arbitrary"),
        name="sb_attn",
    )(bias, q_t, k_t, v_t, u)


def _merge_ffn_body(x_ref, gates_ref, ya_ref, yb_ref, yc_ref, wa_ref, wb_ref, wc_ref, wo_ref,
                    g_ref, wg_ref, wu_ref, wd_ref, o_ref):
    gates = gates_ref[...]
    m = (gates[:, :D_MODEL] * _dot(ya_ref[...].astype(BF16), wa_ref[...])
         + gates[:, D_MODEL:2 * D_MODEL] * _dot(yb_ref[...].astype(BF16), wb_ref[...])
         + gates[:, 2 * D_MODEL:] * _dot(yc_ref[...].astype(BF16), wc_ref[...]))
    x2 = x_ref[...] + _dot(m.astype(BF16), wo_ref[...])
    o_ref[...] = _ffn_value(x2, g_ref, wg_ref, wu_ref, wd_ref)


def _merge_ffn(x, gates, ya, yb, yc, layer, wa, wb, wc, wo, g, wg, wu, wd):
    n, d = x.shape
    tm = min(MERGE_TM, n)

    def row(c):
        return pl.BlockSpec((tm, c), lambda i: (i, 0))

    consts = (wa, wb, wc, wo, g, wg, wu, wd)
    return pl.pallas_call(
        _merge_ffn_body,
        grid=(n // tm,),
        in_specs=[row(d), row(N_GATES), row(SSD_WIDTH), row(SC_WIDTH), row(SB_WIDTH)]
        + [_layer_spec(a, layer) for a in consts],
        out_specs=row(d),
        out_shape=jax.ShapeDtypeStruct((n, d), F32),
        compiler_params=_params("parallel"),
        name="merge_ffn",
    )(x, gates, ya, yb, yc, *consts)


def _dec_ssd_body(z_ref, xbc_ref, dt_ref, sc_ref, cst_ref, scst_ref, st_ref,
                  cw_ref, cb_ref, dtb_ref, alog_ref, dexp_ref, nw_ref, scw_ref, *rest):
    (ya_ref, yb_ref, ncst_ref, nscst_ref, nst_ref,
     xs_s, xdt_s, da_s, dab_s, bm_s, cm_s, xb_s, yt_s) = rest[-13:]
    h = pl.program_id(0)
    nb = z_ref.shape[0]

    @pl.when(h == 0)
    def _():
        xbc = xbc_ref[...]
        cw = cw_ref[...]
        conv = (cst_ref[0] * cw[0:1] + cst_ref[1] * cw[1:2] + cst_ref[2] * cw[2:3]
                + xbc * cw[3:4])
        ncst_ref[0] = cst_ref[1]
        ncst_ref[1] = cst_ref[2]
        ncst_ref[2] = xbc
        act = _silu(conv + cb_ref[...])
        xs = act[:, :SSD_WIDTH]
        xs_s[...] = xs
        c0 = SSD_WIDTH + SSD_GROUPS * SSD_STATE
        for g in range(SSD_GROUPS):
            bm_s[g] = act[:, SSD_WIDTH + g * SSD_STATE:SSD_WIDTH + (g + 1) * SSD_STATE]
            cm_s[g] = act[:, c0 + g * SSD_STATE:c0 + (g + 1) * SSD_STATE]
        dt = _softplus(dt_ref[...] + dtb_ref[...])
        da_s[...] = jnp.exp(dt * (-jnp.exp(alog_ref[...])))
        dt_t = dt.T
        xs_t = xs.T
        for hh in range(SSD_HEADS):
            xdt_s[hh] = xs_t[hh * SSD_HEAD_DIM:(hh + 1) * SSD_HEAD_DIM] * dt_t[hh:hh + 1, :]
        sc = sc_ref[...]
        p = sc[:, SC_WIDTH:2 * SC_WIDTH] * sc[:, 2 * SC_WIDTH:]
        scw = scw_ref[...]
        yb_ref[...] = sc[:, :SC_WIDTH] * (
            scst_ref[0] * scw[0:1] + scst_ref[1] * scw[1:2] + p * scw[2:3])
        nscst_ref[0] = scst_ref[1]
        nscst_ref[1] = p

    lane_t = lax.broadcasted_iota(jnp.int32, (nb, LANES), 1)
    da_col = jnp.sum(jnp.where(lane_t == h, da_s[...], 0.0), axis=1, keepdims=True)
    dab_s[...] = jnp.broadcast_to(da_col, (nb, LANES))

    xt = xdt_s[h]
    for b in range(nb):
        xb_s[b] = jnp.broadcast_to(xt[:, b:b + 1], (SSD_HEAD_DIM, LANES))

    g = h // HEADS_PER_GROUP
    lane = lax.broadcasted_iota(jnp.int32, (SSD_HEAD_DIM, LANES), 1)
    yacc = jnp.zeros((SSD_HEAD_DIM, LANES), F32)
    for b in range(nb):
        hn = st_ref[b] * dab_s[b:b + 1, :] + xb_s[b] * bm_s[g, b:b + 1, :]
        nst_ref[b] = hn
        ycol = jnp.sum(hn * cm_s[g, b:b + 1, :], axis=1, keepdims=True)
        yacc = jnp.where(lane == b, ycol, yacc)
    yt_s[h] = yacc

    @pl.when(h == pl.num_programs(0) - 1)
    def _():
        xs = xs_s[...]
        y = yt_s[...].reshape(SSD_WIDTH, nb).T + xs * dexp_ref[...]
        y = y * _silu(z_ref[...])
        _group_norm_store(ya_ref, y, nw_ref[...])


def _dec_ssd(z, xbc, dt, sc, cst, scst, state, layer, prev_state, cw, cb, dtb, alog, dexp, nw, scw):
    nb = z.shape[0]
    consts = (cw, cb, dtb, alog, dexp, nw, scw)
    st_spec = pl.BlockSpec((None, nb, None, SSD_HEAD_DIM, SSD_STATE), lambda h: (layer, 0, h, 0, 0))
    cst_shape, scst_shape = cst.shape[1:], scst.shape[1:]
    carried = [] if prev_state is None else [prev_state]
    n_in = 7 + len(consts)
    return pl.pallas_call(
        _dec_ssd_body,
        grid=(SSD_HEADS,),
        in_specs=[_const_spec(a.shape) for a in (z, xbc, dt, sc)]
        + [_layer_spec(a, layer) for a in (cst, scst)] + [st_spec]
        + [_layer_spec(a, layer) for a in consts] + [pl.BlockSpec(memory_space=pl.ANY)] * len(carried),
        out_specs=[pl.BlockSpec((nb, SSD_WIDTH), lambda h: (0, 0)),
                   pl.BlockSpec((nb, SC_WIDTH), lambda h: (0, 0)),
                   pl.BlockSpec(cst_shape, lambda h: (0, 0, 0)),
                   pl.BlockSpec(scst_shape, lambda h: (0, 0, 0)),
                   st_spec],
        out_shape=[jax.ShapeDtypeStruct((nb, SSD_WIDTH), F32),
                   jax.ShapeDtypeStruct((nb, SC_WIDTH), F32),
                   jax.ShapeDtypeStruct(cst_shape, F32),
                   jax.ShapeDtypeStruct(scst_shape, F32),
                   jax.ShapeDtypeStruct(state.shape, F32)],
        scratch_shapes=[pltpu.VMEM((nb, SSD_WIDTH), F32),
                        pltpu.VMEM((SSD_HEADS, SSD_HEAD_DIM, nb), F32),
                        pltpu.VMEM((nb, LANES), F32),
                        pltpu.VMEM((nb, LANES), F32),
                        pltpu.VMEM((SSD_GROUPS, nb, SSD_STATE), F32),
                        pltpu.VMEM((SSD_GROUPS, nb, SSD_STATE), F32),
                        pltpu.VMEM((nb, SSD_HEAD_DIM, LANES), F32),
                        pltpu.VMEM((SSD_HEADS, SSD_HEAD_DIM, nb), F32)],
        input_output_aliases={n_in: 4} if carried else {},
        compiler_params=_params("arbitrary"),
        name="dec_ssd",
    )(z, xbc, dt, sc, cst, scst, state, *consts, *carried)


def _page_copies(pt_ref, cache_ref, buf, sem, layer, token, slot):
    n_pages = buf.shape[1]
    return [pltpu.make_async_copy(cache_ref.at[layer, pt_ref[token, j]], buf.at[slot, j], sem.at[slot])
            for j in range(n_pages)]


def _ffn_attn_body(pt_ref, x_ref, g_ref, wg_ref, wu_ref, wd_ref, qt_ref, bias_ref, u_ref, p_ref,
                   kc_ref, vc_ref, o_ref, yo_ref, w_s, kbuf, vbuf, ksem, vsem, *, layer):
    n_pages = kbuf.shape[1]
    b = pl.program_id(0)
    nb = qt_ref.shape[1]
    slot = b % 2

    def fetch(token, into):
        for c in (_page_copies(pt_ref, kc_ref, kbuf, ksem, layer, token, into)
                  + _page_copies(pt_ref, vc_ref, vbuf, vsem, layer, token, into)):
            c.start()

    @pl.when(b == 0)
    def _():
        yo_ref[...] = jnp.zeros_like(yo_ref)
        fetch(0, 0)

    @pl.when(b + 1 < nb)
    def _():
        fetch(b + 1, 1 - slot)

    for c in (_page_copies(pt_ref, kc_ref, kbuf, ksem, layer, b, slot)
              + _page_copies(pt_ref, vc_ref, vbuf, vsem, layer, b, slot)):
        c.wait()

    token = lax.broadcasted_iota(jnp.int32, (SB_WIDTH, nb), 1) == b
    q_col = jnp.sum(jnp.where(token, qt_ref[...], 0.0), axis=1, keepdims=True)
    sub = lax.broadcasted_iota(jnp.int32, (SB_HEADS, LANES), 0)

    def scores(p):
        zp = jnp.zeros((SB_HEADS, LANES), F32)
        for h in range(SB_HEADS):
            prod = kbuf[slot, p, h] * q_col[h * SB_HEAD_DIM:(h + 1) * SB_HEAD_DIM]
            zp = jnp.where(sub == h, jnp.sum(prod, axis=0, keepdims=True), zp)
        return zp

    def weights(z_pages):
        z = jnp.concatenate(z_pages, axis=0) * SB_SCALE + bias_ref[...]
        sp = _softplus(z)
        lk = -sp
        n = z.shape[0]
        tot = jnp.broadcast_to(jnp.sum(lk, axis=1, keepdims=True), (n, LANES))
        tail = _dot_split(lk, u_ref[...], 3) + _dot_split_left(p_ref[...], tot, 3)
        w_s[...] = jnp.exp((z - sp) + tail)

    def values(h):
        acc = jnp.zeros((SB_HEAD_DIM, LANES), F32)
        for p in range(n_pages):
            r = p * SB_HEADS + h
            acc = acc + vbuf[slot, p, h] * w_s[r:r + 1, :]
        return jnp.sum(acc, axis=1, keepdims=True)

    weights([scores(p) for p in range(n_pages)])
    o_col = jnp.concatenate([values(h) for h in range(SB_HEADS)], axis=0)
    yo_ref[...] = jnp.where(token, o_col, yo_ref[...])
    _ffn_body(x_ref, g_ref, wg_ref, wu_ref, wd_ref, o_ref)


def _ffn_attn(x, layer, g, wg, wu, wd, page_table, q_t, cache_kt, cache_vt, bias_col, u, pmat):
    n, d = x.shape
    nb, n_pages = page_table.shape
    page = cache_kt.shape[-1]
    tm = n // nb
    assert tm * nb == n and tm % SUBLANES == 0

    def const(a):
        return pl.BlockSpec(a.shape, lambda b, pt: (0,) * a.ndim, pipeline_mode=pl.Buffered(1))

    row = pl.BlockSpec((tm, d), lambda b, pt: (b, 0))
    grid_spec = pltpu.PrefetchScalarGridSpec(
        num_scalar_prefetch=1,
        grid=(nb,),
        in_specs=[row] + [_layer_spec(a, layer) for a in (g, wg, wu, wd)] + [const(q_t)]
        + [_layer_spec(bias_col, layer), const(u), const(pmat)]
        + [pl.BlockSpec(memory_space=pl.ANY)] * 2,
        out_specs=[row, pl.BlockSpec((SB_WIDTH, nb), lambda b, pt: (0, 0))],
        scratch_shapes=[pltpu.VMEM((n_pages * SB_HEADS, LANES), F32),
                        pltpu.VMEM((2, n_pages, SB_HEADS, SB_HEAD_DIM, page), F32),
                        pltpu.VMEM((2, n_pages, SB_HEADS, SB_HEAD_DIM, page), F32),
                        pltpu.SemaphoreType.DMA((2,)),
                        pltpu.SemaphoreType.DMA((2,))],
    )
    return pl.pallas_call(
        functools.partial(_ffn_attn_body, layer=layer),
        grid_spec=grid_spec,
        out_shape=[jax.ShapeDtypeStruct((n, d), F32), jax.ShapeDtypeStruct((SB_WIDTH, nb), F32)],
        compiler_params=_params("arbitrary"),
        name="ffn_attn",
    )(page_table, x, g, wg, wu, wd, q_t, bias_col, u, pmat, cache_kt, cache_vt)


def _tri01(n, strict_upper_rows):
    r = lax.broadcasted_iota(jnp.int32, (n, n), 0)
    c = lax.broadcasted_iota(jnp.int32, (n, n), 1)
    return ((r > c) if strict_upper_rows else (r >= c)).astype(BF16)


def kernel(x_prompt, x_sample, cache_sb_k, cache_sb_v, page_table, state_ssd, state_ssd_conv, state_sc_conv, ffn1_norm, ffn1_w_gate, ffn1_w_up, ffn1_w_down, mix_norm, w_in, gate_bias, ssd_conv_w, ssd_conv_b, ssd_dt_bias, ssd_a_log, ssd_d, ssd_norm, sc_conv_w, sb_q_norm, sb_k_norm, sb_logit_bias, w_proj_a, w_proj_b, w_proj_c, w_out, ffn2_norm, ffn2_w_gate, ffn2_w_up, ffn2_w_down):
    depth = w_in.shape[0]
    bsz, seq, d = x_prompt.shape
    dec_b = x_sample.shape[0]
    page = cache_sb_k.shape[2]
    n_pages = page_table.shape[1]
    cache_kt = cache_sb_k.transpose(0, 1, 3, 4, 2)
    cache_vt = cache_sb_v.transpose(0, 1, 3, 4, 2)

    tri_incl = _tri01(SSD_CHUNK, False)
    u_tq = _tri01(SB_TQ, True)
    u_page = _tri01(page, True)
    lane = jnp.arange(LANES)
    col = jnp.arange(SSD_WIDTH)
    e_mat = (lane[:, None] == (col[None, :] // SSD_HEAD_DIM)).astype(BF16)
    r = jnp.arange(n_pages * SB_HEADS)
    pmat = ((r[:, None] % SB_HEADS == r[None, :] % SB_HEADS)
            & (r[None, :] // SB_HEADS > r[:, None] // SB_HEADS)).astype(BF16)

    def pad_lanes(a):
        return jnp.pad(a, ((0, 0), (0, LANES - a.shape[1])))

    row3 = lambda a: a[:, None, :]
    w_in_t = w_in.transpose(0, 2, 1).astype(BF16)
    f1 = (row3(ffn1_norm), ffn1_w_gate.astype(BF16), ffn1_w_up.astype(BF16), ffn1_w_down.astype(BF16))
    f2 = (row3(ffn2_norm), ffn2_w_gate.astype(BF16), ffn2_w_up.astype(BF16), ffn2_w_down.astype(BF16))
    ip = (row3(mix_norm), w_in_t, row3(gate_bias), sb_q_norm[:, :, None],
          sb_k_norm[:, :, None])
    ssd_p = (ssd_conv_w, row3(ssd_conv_b), row3(pad_lanes(ssd_dt_bias)), row3(pad_lanes(ssd_a_log)),
             row3(jnp.repeat(ssd_d, SSD_HEAD_DIM, axis=1)), row3(ssd_norm), sc_conv_w)
    mg = (w_proj_a.astype(BF16), w_proj_b.astype(BF16), w_proj_c.astype(BF16), w_out.astype(BF16))
    bias_col = jnp.tile(sb_logit_bias, (1, n_pages))[:, :, None]
    cst_in = state_ssd_conv.transpose(0, 2, 1, 3)
    scst_in = state_sc_conv.transpose(0, 2, 1, 3)

    xp = x_prompt.reshape(bsz * seq, d)
    xs = x_sample.reshape(dec_b * x_sample.shape[1], d)
    kv_p = kv_s = nst = None
    hst, cst, scst, ncst, nscst = [], [], [], [], []
    as_seq = lambda a: a.reshape(bsz, seq, a.shape[-1])
    for l in range(depth):
        s1 = _ffn(xs, l, *f1)
        s_gates, s_z, s_xbc, s_dt, s_sc, s_qt, *kv_s = _inproj(s1, dec_b, depth, l, kv_s, *ip)

        x1, yc_t = _ffn_attn(xp, l, *f1, page_table, s_qt.reshape(SB_WIDTH, dec_b), cache_kt, cache_vt,
                             bias_col, u_page, pmat)
        gates, z, xbc, dt, sc, q_t, *kv_p = _inproj(x1, seq, depth, l, kv_p, *ip)
        ya, yb, h_l, cst8, scst8 = _seqmix(as_seq(z), as_seq(xbc), as_seq(dt), as_seq(sc), l,
                                            *ssd_p, e_mat, tri_incl)
        yc = _sb_attn(q_t, *kv_p, l, sb_logit_bias, u_tq)
        xp = _merge_ffn(x1, gates, ya.reshape(-1, SSD_WIDTH), yb.reshape(-1, SC_WIDTH),
                        yc.reshape(-1, SB_WIDTH), l, *mg, *f2)
        hst.append(h_l)
        cst.append(cst8[:, SUBLANES - (SSD_CONV - 1):])
        scst.append(scst8[:, SUBLANES - (SC_CONV - 1):])

        ya, yb, ncst_l, nscst_l, nst = _dec_ssd(s_z, s_xbc, s_dt, s_sc, cst_in, scst_in,
                                                state_ssd, l, nst, *ssd_p)
        xs = _merge_ffn(s1, s_gates, ya, yb, yc_t.T, l, *mg, *f2)
        ncst.append(ncst_l)
        nscst.append(nscst_l)

    kp, vp = (a.transpose(0, 1, 4, 2, 3) for a in kv_p)
    ks, vs = (a.transpose(0, 4, 1, 2, 3) for a in kv_s)
    return (xp.reshape(bsz, seq, d), xs.reshape(dec_b, x_sample.shape[1], d),
            kp, vp, ks, vs, jnp.stack(hst), nst, jnp.stack(cst),
            jnp.stack(ncst).transpose(0, 2, 1, 3), jnp.stack(scst),
            jnp.stack(nscst).transpose(0, 2, 1, 3))
```
